```python
import math
import jax, jax.numpy as jnp
from jax import lax
import numpy as np

D_MODEL = 1024
BATCH = 8
SEQ = 2048
DEPTH = 1
DEC_BATCH = 32
DEC_SEQ = 1
PAST_LEN = 16384
PAGE_SIZE = 128

D_MIX = D_MODEL
N_HEADS_A = 8
HEAD_DIM_A = 64
N_KV_A = 2
N_HEADS_IDX = 8
HEAD_DIM_IDX = 64
TOPK_MAX = 256
Q_BLOCK = 128
N_HEADS_R = 8
DK_R = 64
DV_R = (D_MIX - N_HEADS_A * HEAD_DIM_A) // N_HEADS_R
HGRN_CHUNK = 64
D_FF = 2816
LN_EPS = 1e-5
RMS_EPS = 1e-6
DN_ALPHA = (2 * DEPTH) ** 0.25
DN_BETA = (8 * DEPTH) ** -0.25

A_Q = N_HEADS_A * HEAD_DIM_A
A_KV = N_KV_A * HEAD_DIM_A
I_Q = N_HEADS_IDX * HEAD_DIM_IDX
I_K = HEAD_DIM_IDX
I_W = N_HEADS_IDX
R_QF = N_HEADS_R * DK_R
R_V = N_HEADS_R * DV_R
SPLIT_SIZES = (A_Q, A_KV, A_KV, I_Q, I_K, I_W, R_QF, R_QF, R_V, R_V)
SPLIT_POINTS = tuple(int(s) for s in np.cumsum(SPLIT_SIZES)[:-1])
D_IN = int(sum(SPLIT_SIZES))

kernel_name = "hymba_dsa_hgrn2_macaron_deepnorm_step"

F32 = jnp.float32


def layer_norm(x, g, b):
    x32 = x.astype(F32)
    mu = jnp.mean(x32, -1, keepdims=True)
    xc = x32 - mu
    var = jnp.mean(xc * xc, -1, keepdims=True)
    return (xc * lax.rsqrt(var + LN_EPS) * g.astype(F32) + b.astype(F32)).astype(x.dtype)


def swiglu(x, wg, wu, wd):
    return (jax.nn.silu(x @ wg) * (x @ wu)) @ wd


def ffn_sublayer(x, wg, wu, wd, g, b):
    return layer_norm(DN_ALPHA * x + 0.5 * swiglu(x, wg, wu, wd), g, b)


def split_projection(h, w_in):
    return jnp.split(h @ w_in, SPLIT_POINTS, axis=-1)


def gather_rows(rows, idx):
    return jax.vmap(lambda a, i: a[i])(rows, idx)


def indexer_scores(qi, wi, kidx, valid):
    logits = jnp.einsum('bqhd,bld->bqhl', qi.astype(F32), kidx.astype(F32)) * (HEAD_DIM_IDX ** -0.5)
    score = jnp.einsum('bqh,bqhl->bql', wi.astype(F32) * (N_HEADS_IDX ** -0.5), jax.nn.relu(logits))
    return jnp.where(valid, score, -jnp.inf)


def sparse_attend(q, k_sel, v_sel, valid):
    B, Q, H, DH = q.shape
    G = k_sel.shape[3]
    qg = q.astype(F32).reshape(B, Q, G, H // G, DH)
    s = jnp.einsum('bqgrd,bqkgd->bqgrk', qg, k_sel.astype(F32)) * (DH ** -0.5)
    s = jnp.where(valid[:, :, None, None, :], s, -jnp.inf)
    p = jax.nn.softmax(s, axis=-1)
    o = jnp.einsum('bqgrk,bqkgd->bqgrd', p, v_sel.astype(F32))
    return o.reshape(B, Q, H * DH).astype(q.dtype)


def hgrn_lower_bound(lb_param, layer):
    return jnp.cumsum(jax.nn.softmax(lb_param.astype(F32), axis=0), axis=0)[layer]


def hgrn2_recurrence(q, k, v, logf, s0, chunk):
    B, T, H, DK = q.shape
    DV = v.shape[-1]
    n = T // chunk

    def to_chunks(a):
        return a.reshape(B, n, chunk, H, a.shape[-1]).transpose(1, 0, 3, 2, 4)

    tri = jnp.tril(jnp.ones((chunk, chunk), bool))[None, None, :, :, None]

    def step(S, inp):
        qc, kc, vc, gc = inp
        b = jnp.cumsum(gc, axis=2)
        diff = b[:, :, :, None, :] - b[:, :, None, :, :]
        decay = jnp.exp(jnp.where(tri, diff, -jnp.inf))
        A = jnp.einsum('bhtd,bhsd,bhtsd->bhts', qc, kc, decay)
        o = jnp.einsum('bhts,bhse->bhte', A, vc) + jnp.einsum('bhtd,bhde->bhte', qc * jnp.exp(b), S)
        b_last = b[:, :, -1:, :]
        S = jnp.exp(b_last[:, :, 0, :])[..., None] * S + jnp.einsum('bhsd,bhse->bhde', kc * jnp.exp(b_last - b), vc)
        return S, o

    S, o = lax.scan(step, s0, (to_chunks(q), to_chunks(k), to_chunks(v), to_chunks(logf)))
    o = o.transpose(1, 0, 3, 2, 4).reshape(B, T, H, DV)
    return o, S


def hgrn2_branch(rq, rf, ri, rg, lb, norm_g, s0, chunk):
    B, T, _ = rq.shape
    f = lb + (1.0 - lb) * jax.nn.sigmoid(rf.astype(F32))
    logf = jnp.log(f)
    kk = 1.0 - f
    shp = (B, T, N_HEADS_R, DK_R)
    o, s_new = hgrn2_recurrence(rq.astype(F32).reshape(shp), kk.reshape(shp),
                                ri.astype(F32).reshape(B, T, N_HEADS_R, DV_R),
                                logf.reshape(shp), s0.astype(F32), chunk)
    o = o * lax.rsqrt(jnp.mean(o * o, -1, keepdims=True) + RMS_EPS)
    o = o.reshape(B, T, R_V) * norm_g.astype(F32) * jax.nn.silu(rg.astype(F32))
    return o.astype(rq.dtype), s_new.astype(rq.dtype)


def mixer_prompt(h, w_in, w_out, lb, norm_g):
    B, S, _ = h.shape
    q, k, v, qi, ki, wi, rq, rf, ri, rg = split_projection(h, w_in)
    q = q.reshape(B, S, N_HEADS_A, HEAD_DIM_A)
    k = k.reshape(B, S, N_KV_A, HEAD_DIM_A)
    v = v.reshape(B, S, N_KV_A, HEAD_DIM_A)
    qi = qi.reshape(B, S, N_HEADS_IDX, HEAD_DIM_IDX)
    n_sel = min(TOPK_MAX, S // 4)
    qbs = min(Q_BLOCK, S)
    nb = S // qbs
    key_pos = jnp.arange(S)

    def blocks(a):
        return a.reshape((B, nb, qbs) + a.shape[2:]).swapaxes(0, 1)

    def attend_block(args):
        q_b, qi_b, w_b, start = args
        pos = start + jnp.arange(qbs)
        valid = key_pos[None, :] <= pos[:, None]
        score = indexer_scores(qi_b, w_b, ki, valid[None])
        _, idx = lax.top_k(score, n_sel)
        flat = idx.reshape(B, qbs * n_sel)
        k_sel = gather_rows(k, flat).reshape(B, qbs, n_sel, N_KV_A, HEAD_DIM_A)
        v_sel = gather_rows(v, flat).reshape(B, qbs, n_sel, N_KV_A, HEAD_DIM_A)
        return sparse_attend(q_b, k_sel, v_sel, idx <= pos[None, :, None])

    starts = jnp.arange(nb, dtype=jnp.int32) * qbs
    o_attn = lax.map(attend_block, (blocks(q), blocks(qi), blocks(wi), starts))
    o_attn = o_attn.swapaxes(0, 1).reshape(B, S, A_Q)
    s0 = jnp.zeros((B, N_HEADS_R, DK_R, DV_R), F32)
    o_r, s_new = hgrn2_branch(rq, rf, ri, rg, lb, norm_g, s0, min(HGRN_CHUNK, S))
    m = jnp.concatenate([o_attn, o_r], axis=-1) @ w_out
    return m, k, v, ki, s_new


def mixer_sample(h, cache_k, cache_v, cache_kidx, state, page_table, w_in, w_out, lb, norm_g):
    B, T, _ = h.shape
    q, k, v, qi, ki, wi, rq, rf, ri, rg = split_projection(h, w_in)
    q = q.reshape(B, T, N_HEADS_A, HEAD_DIM_A)
    k = k.reshape(B, T, N_KV_A, HEAD_DIM_A)
    v = v.reshape(B, T, N_KV_A, HEAD_DIM_A)
    qi = qi.reshape(B, T, N_HEADS_IDX, HEAD_DIM_IDX)
    n_pages = page_table.shape[1]
    past = n_pages * PAGE_SIZE
    L = past + T
    n_sel = min(TOPK_MAX, L // 4)
    ki_past = cache_kidx[page_table].reshape(B, past, HEAD_DIM_IDX).astype(ki.dtype)
    ki_all = jnp.concatenate([ki_past, ki], axis=1)
    pos = past + jnp.arange(T)
    valid = jnp.arange(L)[None, :] <= pos[:, None]
    score = indexer_scores(qi, wi, ki_all, valid[None])
    _, idx = lax.top_k(score, n_sel)
    in_past = (idx < past)[..., None, None]
    pidx = jnp.minimum(idx, past - 1)
    phys = page_table[jnp.arange(B)[:, None, None], pidx // PAGE_SIZE]
    off = pidx % PAGE_SIZE
    nidx = jnp.clip(idx - past, 0, T - 1).reshape(B, T * n_sel)
    k_new = gather_rows(k, nidx).reshape(B, T, n_sel, N_KV_A, HEAD_DIM_A)
    v_new = gather_rows(v, nidx).reshape(B, T, n_sel, N_KV_A, HEAD_DIM_A)
    k_sel = jnp.where(in_past, cache_k[phys, off].astype(k.dtype), k_new)
    v_sel = jnp.where(in_past, cache_v[phys, off].astype(v.dtype), v_new)
    o_attn = sparse_attend(q, k_sel, v_sel, idx <= pos[None, :, None])
    o_r, s_new = hgrn2_branch(rq, rf, ri, rg, lb, norm_g, state, T)
    m = jnp.concatenate([o_attn, o_r], axis=-1) @ w_out
    return m, k, v, ki, s_new


def setup_inputs(seed: int = 0) -> dict:
    key = jax.random.key(seed)
    ks = jax.random.split(key, 32)
    n_pages = PAST_LEN // PAGE_SIZE
    n_used = DEC_BATCH * n_pages
    n_phys = n_used + max(1, n_used // 4)

    def nrm(k, shape, scale=1.0):
        return jax.random.normal(k, shape, F32) * scale

    def gain(k, shape):
        return 1.0 + 0.02 * jax.random.normal(k, shape, F32)

    page_table = jax.random.permutation(ks[0], n_phys)[:n_used].reshape(DEC_BATCH, n_pages).astype(jnp.int32)
    return {
        "x_prompt": nrm(ks[1], (BATCH, SEQ, D_MODEL)),
        "x_sample": nrm(ks[2], (DEC_BATCH, DEC_SEQ, D_MODEL)),
        "cache_k": nrm(ks[3], (DEPTH, n_phys, PAGE_SIZE, N_KV_A, HEAD_DIM_A)),
        "cache_v": nrm(ks[4], (DEPTH, n_phys, PAGE_SIZE, N_KV_A, HEAD_DIM_A)),
        "cache_kidx": nrm(ks[5], (DEPTH, n_phys, PAGE_SIZE, HEAD_DIM_IDX)),
        "state_hgrn": nrm(ks[6], (DEPTH, DEC_BATCH, N_HEADS_R, DK_R, DV_R), 0.5),
        "page_table": page_table,
        "w_in": nrm(ks[7], (DEPTH, D_MODEL, D_IN), D_MODEL ** -0.5),
        "w_out": nrm(ks[8], (DEPTH, D_MIX, D_MODEL), DN_BETA * D_MIX ** -0.5),
        "hgrn_lb": nrm(ks[9], (DEPTH + 1, R_QF)),
        "hgrn_norm_g": gain(ks[10], (DEPTH, R_V)),
        "ffn1_wg": nrm(ks[11], (DEPTH, D_MODEL, D_FF), D_MODEL ** -0.5),
        "ffn1_wu": nrm(ks[12], (DEPTH, D_MODEL, D_FF), D_MODEL ** -0.5),
        "ffn1_wd": nrm(ks[13], (DEPTH, D_FF, D_MODEL), DN_BETA * D_FF ** -0.5),
        "ffn2_wg": nrm(ks[14], (DEPTH, D_MODEL, D_FF), D_MODEL ** -0.5),
        "ffn2_wu": nrm(ks[15], (DEPTH, D_MODEL, D_FF), D_MODEL ** -0.5),
        "ffn2_wd": nrm(ks[16], (DEPTH, D_FF, D_MODEL), DN_BETA * D_FF ** -0.5),
        "ln1_g": gain(ks[17], (DEPTH, D_MODEL)),
        "ln1_b": nrm(ks[18], (DEPTH, D_MODEL), 0.02),
        "ln2_g": gain(ks[19], (DEPTH, D_MODEL)),
        "ln2_b": nrm(ks[20], (DEPTH, D_MODEL), 0.02),
        "ln3_g": gain(ks[21], (DEPTH, D_MODEL)),
        "ln3_b": nrm(ks[22], (DEPTH, D_MODEL), 0.02),
    }


def reference(x_prompt, x_sample, cache_k, cache_v, cache_kidx, state_hgrn, page_table,
              w_in, w_out, hgrn_lb, hgrn_norm_g,
              ffn1_wg, ffn1_wu, ffn1_wd, ffn2_wg, ffn2_wu, ffn2_wd,
              ln1_g, ln1_b, ln2_g, ln2_b, ln3_g, ln3_b):
    yp, ys = x_prompt, x_sample
    kp_l, vp_l, kip_l, sp_l = [], [], [], []
    ks_l, vs_l, kis_l, ss_l = [], [], [], []
    for l in range(DEPTH):
        lb = hgrn_lower_bound(hgrn_lb, l)
        hp = ffn_sublayer(yp, ffn1_wg[l], ffn1_wu[l], ffn1_wd[l], ln1_g[l], ln1_b[l])
        hs = ffn_sublayer(ys, ffn1_wg[l], ffn1_wu[l], ffn1_wd[l], ln1_g[l], ln1_b[l])
        mp, kp, vp, kip, sp = mixer_prompt(hp, w_in[l], w_out[l], lb, hgrn_norm_g[l])
        ms, ksn, vsn, kisn, ssn = mixer_sample(hs, cache_k[l], cache_v[l], cache_kidx[l], state_hgrn[l],
                                            page_table, w_in[l], w_out[l], lb, hgrn_norm_g[l])
        hp = layer_norm(DN_ALPHA * hp + mp, ln2_g[l], ln2_b[l])
        hs = layer_norm(DN_ALPHA * hs + ms, ln2_g[l], ln2_b[l])
        yp = ffn_sublayer(hp, ffn2_wg[l], ffn2_wu[l], ffn2_wd[l], ln3_g[l], ln3_b[l])
        ys = ffn_sublayer(hs, ffn2_wg[l], ffn2_wu[l], ffn2_wd[l], ln3_g[l], ln3_b[l])
        kp_l.append(kp); vp_l.append(vp); kip_l.append(kip); sp_l.append(sp)
        ks_l.append(ksn); vs_l.append(vsn); kis_l.append(kisn); ss_l.append(ssn)
    k_prompt = jnp.stack(kp_l)
    v_prompt = jnp.stack(vp_l)
    kidx_prompt = jnp.stack(kip_l)
    state_prompt = jnp.stack(sp_l)
    k_sample = jnp.stack(ks_l)
    v_sample = jnp.stack(vs_l)
    kidx_sample = jnp.stack(kis_l)
    state_sample = jnp.stack(ss_l)
    return (yp, ys, k_prompt, v_prompt, kidx_prompt, state_prompt, k_sample, v_sample, kidx_sample, state_sample)
```

```python
import functools

import numpy as np
import jax
import jax.numpy as jnp
from jax import lax
from jax.experimental import pallas as pl
from jax.experimental.pallas import tpu as pltpu

F32 = jnp.float32
BF16 = jnp.bfloat16
I32 = jnp.int32

DEPTH = 1
N_HEADS_A = 8
N_KV_A = 2
HEAD_DIM = 64
N_HEADS_IDX = 8
N_HEADS_R = 8
TOPK_MAX = 256
PAGE_SIZE = 128
LN_EPS = 1e-5
RMS_EPS = 1e-6
DN_ALPHA = (2 * DEPTH) ** 0.25

LANES = 128
SUBLANES = 8
VMEM_LIMIT_BYTES = 56 * 1024 * 1024

NEG_INF = float("-inf")
INT_MIN = -(2 ** 31)
_NEG_INF_BITS = int(np.float32(NEG_INF).view(np.int32))
NEG_INF_KEY = _NEG_INF_BITS ^ 0x7FFFFFFF


def _dot(a, b):
    return jnp.dot(a, b, preferred_element_type=F32)


def _dot_nt(a, b):
    return lax.dot_general(a, b, (((1,), (1,)), ((), ())), preferred_element_type=F32)


def _dot_tn(a, b):
    return lax.dot_general(a, b, (((0,), (0,)), ((), ())), preferred_element_type=F32)


def _split3(x):
    hi = x.astype(BF16)
    r1 = x - hi.astype(F32)
    mid = r1.astype(BF16)
    lo = (r1 - mid.astype(F32)).astype(BF16)
    return hi, mid, lo


def _dot_exact_lhs(m_bf, x):
    hi, mid, lo = _split3(x)
    return _dot(m_bf, hi) + _dot(m_bf, mid) + _dot(m_bf, lo)


def _dot_exact_rhs(x, m_bf):
    hi, mid, lo = _split3(x)
    return _dot(hi, m_bf) + _dot(mid, m_bf) + _dot(lo, m_bf)


def _layer_norm(x, g, b):
    mu = jnp.mean(x, -1, keepdims=True)
    xc = x - mu
    var = jnp.mean(xc * xc, -1, keepdims=True)
    return xc * lax.rsqrt(var + LN_EPS) * g + b


def _ffn_kernel(*refs, n_chunks, with_mix):
    if with_mix:
        (h_ref, oa_ref, or_ref, woa_ref, wor_ref, g2_ref, b2_ref,
         wg_ref, wu_ref, wd_ref, g_ref, b_ref, o_ref, acc_ref, x_ref) = refs
        m = _dot(oa_ref[...], woa_ref[...]) + _dot(or_ref[...], wor_ref[...])
        x_ref[...] = _layer_norm(DN_ALPHA * h_ref[...] + m, g2_ref[...], b2_ref[...])
    else:
        x_ref, wg_ref, wu_ref, wd_ref, g_ref, b_ref, o_ref, acc_ref = refs
    xb = x_ref[...].astype(BF16)
    acc_ref[...] = jnp.zeros_like(acc_ref)

    def body(c, carry):
        gg = _dot(xb, wg_ref[c])
        uu = _dot(xb, wu_ref[c])
        hh = (gg * jax.nn.sigmoid(gg) * uu).astype(BF16)
        acc_ref[...] += _dot(hh, wd_ref[c])
        return carry

    lax.fori_loop(0, n_chunks, body, 0)
    o_ref[...] = _layer_norm(DN_ALPHA * x_ref[...] + 0.5 * acc_ref[...], g_ref[...], b_ref[...])


def _const_spec(shape):
    nd = len(shape)
    return pl.BlockSpec(shape, lambda i, _nd=nd: (0,) * _nd, pipeline_mode=pl.Buffered(1))


def _ffn_weights(wg, wu, wd, fc):
    d, ff = wg.shape
    n = ff // fc
    wg3 = wg.astype(BF16).reshape(d, n, fc).transpose(1, 0, 2)
    wu3 = wu.astype(BF16).reshape(d, n, fc).transpose(1, 0, 2)
    wd3 = wd.astype(BF16).reshape(n, fc, d)
    return wg3, wu3, wd3


def _ffn(x, w3, g, b, tm, mix=None):
    wg3, wu3, wd3 = w3
    n_rows, d = x.shape
    n_chunks, _, fc = wg3.shape
    assert n_rows % tm == 0
    row = lambda w: pl.BlockSpec((tm, w), lambda i: (i, 0))
    vec = lambda a: a.reshape(1, -1).astype(F32)
    args, specs = [x], [row(d)]
    scratch = [pltpu.VMEM((tm, d), F32)]
    if mix is not None:
        oa, orr, woa, wor, g2, b2 = mix
        args += [oa, orr, woa, wor, vec(g2), vec(b2)]
        specs += [row(oa.shape[1]), row(orr.shape[1]), _const_spec(woa.shape), _const_spec(wor.shape),
                  _const_spec((1, d)), _const_spec((1, d))]
        scratch.append(pltpu.VMEM((tm, d), F32))
    args += [wg3, wu3, wd3, vec(g), vec(b)]
    specs += [_const_spec(wg3.shape), _const_spec(wu3.shape), _const_spec(wd3.shape),
              _const_spec((1, d)), _const_spec((1, d))]
    return pl.pallas_call(
        functools.partial(_ffn_kernel, n_chunks=n_chunks, with_mix=mix is not None),
        grid=(n_rows // tm,),
        in_specs=specs,
        out_specs=row(d),
        out_shape=jax.ShapeDtypeStruct((n_rows, d), F32),
        scratch_shapes=scratch,
        compiler_params=pltpu.CompilerParams(dimension_semantics=("arbitrary",),
                                             vmem_limit_bytes=VMEM_LIMIT_BYTES),
        name="ffn_mix" if mix is not None else "ffn",
    )(*args)


A_Q = N_HEADS_A * HEAD_DIM
A_KV = N_KV_A * HEAD_DIM
I_Q = N_HEADS_IDX * HEAD_DIM
R_W = N_HEADS_R * HEAD_DIM
_SPLIT = (A_Q, A_KV, A_KV, I_Q, HEAD_DIM, N_HEADS_IDX, R_W, R_W, R_W, R_W)
_OFF = tuple(int(v) for v in np.cumsum((0,) + _SPLIT))
GROUP = N_HEADS_A // N_KV_A


def _q_perm():
    idx = []
    for j in range(GROUP):
        idx += list(range(j * HEAD_DIM, (j + 1) * HEAD_DIM))
        idx += list(range((GROUP + j) * HEAD_DIM, (GROUP + j + 1) * HEAD_DIM))
    return np.asarray(idx, np.int32)


def _proj_weight(w_in):
    seg = lambda i: w_in[:, _OFF[i]:_OFF[i + 1]]
    q = seg(0)[:, _q_perm()]
    ki = seg(4)
    wi = jnp.pad(seg(5), ((0, 0), (0, LANES - N_HEADS_IDX)))
    cols = [q, seg(1), seg(2), seg(3), ki, ki, wi, seg(6), seg(7), seg(8), seg(9)]
    return jnp.concatenate(cols, axis=1).astype(BF16)


def _proj_kernel(h_ref, w_ref, q_ref, k_ref, v_ref, kb_ref, vb_ref, qi_ref, ki_ref, kk_ref, wi_ref,
                 rq_ref, rf_ref, ri_ref, rg_ref):
    hb = h_ref[...].astype(BF16)
    col = [0]

    def take(width):
        out = _dot(hb, w_ref[:, col[0]:col[0] + width])
        col[0] += width
        return out

    scale = HEAD_DIM ** -0.5
    q_ref[...] = (take(A_Q) * scale).astype(BF16)
    k = take(A_KV)
    k_ref[...] = k
    kb_ref[...] = k.astype(BF16)
    v = take(A_KV)
    v_ref[...] = v
    vb_ref[...] = v.astype(BF16)
    qi_ref[...] = (take(I_Q) * scale).astype(BF16)
    kk = take(LANES)
    ki_ref[...] = kk[:, :HEAD_DIM]
    kk_ref[...] = kk.astype(BF16)
    wi_ref[...] = take(LANES) * (N_HEADS_IDX ** -0.5)
    rq_ref[...] = take(R_W)
    rf_ref[...] = take(R_W)
    ri_ref[...] = take(R_W)
    rg_ref[...] = take(R_W)


def _proj(h, w_cat, tm):
    n_rows, d = h.shape
    assert n_rows % tm == 0
    row = lambda w: pl.BlockSpec((tm, w), lambda i: (i, 0))
    outs = [(A_Q, BF16), (A_KV, F32), (A_KV, F32), (A_KV, BF16), (A_KV, BF16), (I_Q, BF16),
            (HEAD_DIM, F32), (LANES, BF16), (LANES, F32), (R_W, F32), (R_W, F32), (R_W, F32), (R_W, F32)]
    return pl.pallas_call(
        _proj_kernel,
        grid=(n_rows // tm,),
        in_specs=[row(d), _const_spec(w_cat.shape)],
        out_specs=[row(w) for w, _ in outs],
        out_shape=[jax.ShapeDtypeStruct((n_rows, w), dt) for w, dt in outs],
        compiler_params=pltpu.CompilerParams(dimension_semantics=("arbitrary",),
                                             vmem_limit_bytes=VMEM_LIMIT_BYTES),
        name="in_proj",
    )(h, w_cat)


def _sort_key(score):
    bits = pltpu.bitcast(score + 0.0, I32)
    return bits ^ ((bits >> 31) & jnp.int32(0x7FFFFFFF))


def _count(mask):
    return jnp.sum(jnp.where(mask, 1.0, 0.0), axis=1, keepdims=True)


def _kth_largest_key(key_ref, n_sel):
    rows = key_ref.shape[0]

    def body(i, t_u):
        bit = jnp.left_shift(jnp.int32(1), 31 - i)
        cand_u = t_u | bit
        cnt = _count(key_ref[...] >= (cand_u ^ jnp.int32(INT_MIN)))
        return jnp.where(cnt >= n_sel, cand_u, t_u)

    t_u = lax.fori_loop(0, 32, body, jnp.zeros((rows, 1), I32))
    return t_u ^ jnp.int32(INT_MIN)


def _tie_cut(key_ref, thr, idx, n_sel, n_bits):
    rows, width = key_ref.shape
    need = n_sel - _count(key_ref[...] > thr)

    def body(i, j):
        cand = j | jnp.left_shift(jnp.int32(1), n_bits - 1 - i)
        cnt = _count((key_ref[...] == thr) & (idx < cand))
        return jnp.where(cnt <= need, cand, j)

    return lax.fori_loop(0, n_bits, body, jnp.zeros((rows, 1), I32))


def _select_mask(key_ref, idx, n_sel, n_bits, cut_ref):
    thr = _kth_largest_key(key_ref, n_sel)
    keys = key_ref[...]
    eq = keys == thr
    excess = ((_count(keys > thr) + _count(eq)) > n_sel) & (thr != NEG_INF_KEY)
    cut_ref[...] = jnp.full(cut_ref.shape, 1 << n_bits, I32)

    @pl.when(jnp.max(jnp.where(excess, 1.0, 0.0)) > 0.0)
    def _():
        cut_ref[...] = _tie_cut(key_ref, thr, idx, n_sel, n_bits + 1)

    return (keys > thr) | (eq & (idx < cut_ref[...]))


def _attn_prompt_kernel(q_ref, qi_ref, wi_ref, kk_ref, k_ref, v_ref, o_ref, key_ref, cut_ref, *, n_sel):
    tq = q_ref.shape[0]
    s_len = k_ref.shape[0]
    n_bits = int(s_len).bit_length() - 1
    q_pos = pl.program_id(1) * tq + lax.broadcasted_iota(I32, (tq, 1), 0)
    key_pos = lax.broadcasted_iota(I32, (tq, s_len), 1)
    causal = key_pos <= q_pos
    low = lax.broadcasted_iota(I32, (1, LANES), 1) < HEAD_DIM
    zero = jnp.zeros((), BF16)

    kk = kk_ref[...]
    score = jnp.zeros((tq, s_len), F32)
    for j in range(I_Q // LANES):
        blk = qi_ref[:, j * LANES:(j + 1) * LANES]
        for half in range(2):
            h = 2 * j + half
            qm = jnp.where(low if half == 0 else ~low, blk, zero)
            score = score + wi_ref[:, h:h + 1] * jnp.maximum(_dot_nt(qm, kk), 0.0)
    key_ref[...] = _sort_key(jnp.where(causal, score, NEG_INF))
    sel = _select_mask(key_ref, key_pos, n_sel, n_bits, cut_ref) & causal
    bias = jnp.where(sel, 0.0, NEG_INF)

    kb = k_ref[...]
    vb = v_ref[...]
    for j in range(GROUP):
        blk = q_ref[:, j * LANES:(j + 1) * LANES]
        outs = []
        for half in range(2):
            qm = jnp.where(low if half == 0 else ~low, blk, zero)
            s = _dot_nt(qm, kb) + bias
            m = jnp.max(s, axis=1, keepdims=True)
            p = jnp.exp(s - m)
            l = jnp.sum(p, axis=1, keepdims=True)
            outs.append(_dot(p.astype(BF16), vb) / l)
        o_ref[:, j * LANES:(j + 1) * LANES] = jnp.where(low, outs[0], outs[1]).astype(o_ref.dtype)


def _attn_prompt(q, qi, wi, kk, kb, vb, batch, tq):
    n_rows = q.shape[0]
    s_len = n_rows // batch
    assert s_len & (s_len - 1) == 0 and s_len % tq == 0
    n_sel = min(TOPK_MAX, s_len // 4)
    nq = s_len // tq
    tile = lambda w: pl.BlockSpec((tq, w), lambda b, i: (b * nq + i, 0))
    full = lambda w: pl.BlockSpec((s_len, w), lambda b, i: (b, 0))
    return pl.pallas_call(
        functools.partial(_attn_prompt_kernel, n_sel=n_sel),
        grid=(batch, nq),
        in_specs=[tile(A_Q), tile(I_Q), tile(LANES), full(LANES), full(A_KV), full(A_KV)],
        out_specs=tile(A_Q),
        out_shape=jax.ShapeDtypeStruct((n_rows, A_Q), BF16),
        scratch_shapes=[pltpu.VMEM((tq, s_len), I32), pltpu.VMEM((tq, 1), I32)],
        compiler_params=pltpu.CompilerParams(dimension_semantics=("arbitrary", "arbitrary"),
                                             vmem_limit_bytes=VMEM_LIMIT_BYTES),
        name="attn_prompt",
    )(q, qi, wi, kk, kb, vb)


def _hgrn_consts(c):
    t = np.arange(c)
    tri = (t[None, :] <= t[:, None]).astype(np.float32)
    sels, masks = [], []
    hs = c // 2
    while hs >= 1:
        blk = t // (2 * hs)
        ref_row = blk * 2 * hs + hs - 1
        sels.append((t[None, :] == ref_row[:, None]).astype(np.float32))
        right = (t % (2 * hs)) >= hs
        masks.append(((blk[:, None] == blk[None, :]) & right[:, None] & ~right[None, :]).astype(np.float32))
        hs //= 2
    return (jnp.asarray(tri, BF16), jnp.asarray(np.concatenate(sels, 0), BF16),
            jnp.asarray(np.stack(masks, 0), F32))


def _head_block_ones():
    d = np.arange(LANES) // HEAD_DIM
    return (d[:, None] == d[None, :]).astype(np.float32)


def _hgrn_prompt_kernel(rq_ref, rf_ref, ri_ref, rg_ref, lb_ref, ng_ref, tri_ref, sel_ref, msk_ref, bo_ref,
                        o_ref, s_out_ref, st_ref):
    c = rq_ref.shape[0]
    n_levels = msk_ref.shape[0]
    step = pl.program_id(2)

    @pl.when(step == 0)
    def _():
        st_ref[...] = jnp.zeros_like(st_ref)

    lb = lb_ref[...]
    f = lb + (1.0 - lb) * jax.nn.sigmoid(rf_ref[...])
    logf = jnp.log(f)
    q = rq_ref[...]
    k = 1.0 - f
    v = ri_ref[...]
    vb = v.astype(BF16)
    bo = bo_ref[...]
    same_head = bo > 0.5
    low = lax.broadcasted_iota(I32, (1, LANES), 1) < HEAD_DIM

    b = _dot_exact_lhs(tri_ref[...], logf)
    st = st_ref[...]
    qd = (q * jnp.exp(b)).astype(BF16)
    o = _dot_nt(qd, st.astype(BF16))
    o = o + _dot((q * k).astype(BF16), bo.astype(BF16)) * v

    a0 = jnp.zeros((c, c), F32)
    a1 = jnp.zeros((c, c), F32)
    hi, mid, lo = _split3(b)
    for lv in range(n_levels):
        sel = sel_ref[lv * c:(lv + 1) * c, :]
        e = _dot(sel, hi) + _dot(sel, mid) + _dot(sel, lo)
        diff = b - e
        qdl = q * jnp.exp(jnp.minimum(diff, 0.0))
        kdl = (k * jnp.exp(jnp.minimum(-diff, 0.0))).astype(BF16)
        msk = msk_ref[lv] > 0.5
        a0 = a0 + jnp.where(msk, _dot_nt(jnp.where(low, qdl, 0.0).astype(BF16), kdl), 0.0)
        a1 = a1 + jnp.where(msk, _dot_nt(jnp.where(low, 0.0, qdl).astype(BF16), kdl), 0.0)
    o = o + jnp.where(low, _dot(a0.astype(BF16), vb), _dot(a1.astype(BF16), vb))

    b_last = b[c - 1:c, :]
    kdec = (k * jnp.exp(jnp.minimum(b_last - b, 0.0))).astype(BF16)
    st_new = st * jnp.exp(b_last) + jnp.where(same_head, _dot_tn(vb, kdec), 0.0)
    st_ref[...] = st_new

    ms = _dot_exact_rhs(o * o, bo.astype(BF16)) * (1.0 / HEAD_DIM)
    g = rg_ref[...]
    o_ref[...] = (o * lax.rsqrt(ms + RMS_EPS) * ng_ref[...] * (g * jax.nn.sigmoid(g))).astype(o_ref.dtype)

    @pl.when(step == pl.num_programs(2) - 1)
    def _():
        s_out_ref[...] = st_new.T


def _hgrn_prompt(rq, rf, ri, rg, lb, norm_g, batch, chunk):
    n_rows = rq.shape[0]
    t_len = n_rows // batch
    assert t_len % chunk == 0 and chunk & (chunk - 1) == 0
    nc = t_len // chunk
    n_pairs = R_W // LANES
    tri, sel, msk = _hgrn_consts(chunk)
    bo = jnp.asarray(_head_block_ones(), F32)
    tile = pl.BlockSpec((chunk, LANES), lambda b, p, i: (b * nc + i, p))
    vec = pl.BlockSpec((1, LANES), lambda b, p, i: (0, p))
    const = lambda a: pl.BlockSpec(a.shape, lambda b, p, i, _nd=a.ndim: (0,) * _nd)
    o, s_pairs = pl.pallas_call(
        _hgrn_prompt_kernel,
        grid=(batch, n_pairs, nc),
        in_specs=[tile, tile, tile, tile, vec, vec, const(tri), const(sel), const(msk), const(bo)],
        out_specs=[tile, pl.BlockSpec((None, None, LANES, LANES), lambda b, p, i: (b, p, 0, 0))],
        out_shape=[jax.ShapeDtypeStruct((n_rows, R_W), BF16),
                   jax.ShapeDtypeStruct((batch, n_pairs, LANES, LANES), F32)],
        scratch_shapes=[pltpu.VMEM((LANES, LANES), F32)],
        compiler_params=pltpu.CompilerParams(dimension_semantics=("arbitrary", "arbitrary", "arbitrary"),
                                             vmem_limit_bytes=VMEM_LIMIT_BYTES),
        name="hgrn_prompt",
    )(rq, rf, ri, rg, lb.reshape(1, -1), norm_g.reshape(1, -1), tri, sel, msk, bo)
    s0 = s_pairs[:, :, :HEAD_DIM, :HEAD_DIM]
    s1 = s_pairs[:, :, HEAD_DIM:, HEAD_DIM:]
    return o, jnp.stack([s0, s1], axis=2).reshape(batch, N_HEADS_R, HEAD_DIM, HEAD_DIM)


def _column(row, eye):
    return jnp.sum(jnp.where(eye, row, 0.0), axis=1, keepdims=True)


def _eye(n):
    return lax.broadcasted_iota(I32, (n, n), 0) == lax.broadcasted_iota(I32, (n, n), 1)


def _fetch_pages(pt_ref, b, n_pages, src_ref, dst_ref, sem):
    def start(p, carry):
        pltpu.make_async_copy(src_ref.at[pt_ref[b, p]], dst_ref.at[p], sem).start()
        return carry

    lax.fori_loop(0, n_pages, start, 0)


def _wait_pages(n_pages, src_ref, dst_ref, sem):
    def wait(p, carry):
        pltpu.make_async_copy(src_ref.at[0], dst_ref.at[p], sem).wait()
        return carry

    lax.fori_loop(0, n_pages, wait, 0)


def _score_sample_kernel(pt_ref, qi_ref, wi_ref, kin_ref, cache_ref, o_ref, kbuf, sem):
    b = pl.program_id(0)
    n_pages = kbuf.shape[0]
    _fetch_pages(pt_ref, b, n_pages, cache_ref, kbuf, sem)
    qi = qi_ref[...]
    wcol = _column(wi_ref[pl.ds(b, 1), :][:, :N_HEADS_IDX], _eye(N_HEADS_IDX))
    lane0 = lax.broadcasted_iota(I32, (1, PAGE_SIZE), 1) == 0
    ki_new = kin_ref[pl.ds(b, 1), :].astype(BF16).astype(F32)
    lg_new = jnp.sum(qi.astype(F32) * ki_new, axis=1, keepdims=True)
    sc_new = jnp.sum(wcol * jnp.maximum(lg_new, 0.0), axis=0, keepdims=True)
    o_ref[...] = jnp.full(o_ref.shape, NEG_INF, F32)
    o_ref[pl.ds(n_pages, 1), :] = jnp.where(lane0, sc_new, NEG_INF)
    _wait_pages(n_pages, cache_ref, kbuf, sem)

    def page(p, carry):
        lg = _dot(qi, kbuf[p].astype(BF16))
        o_ref[pl.ds(p, 1), :] = jnp.sum(wcol * jnp.maximum(lg, 0.0), axis=0, keepdims=True)
        return carry

    lax.fori_loop(0, n_pages, page, 0)


def _score_sample(page_table, qi3, wi, ki_new, cache_kiT, rows_pad):
    batch, n_pages = page_table.shape
    return pl.pallas_call(
        _score_sample_kernel,
        grid_spec=pltpu.PrefetchScalarGridSpec(
            num_scalar_prefetch=1,
            grid=(batch,),
            in_specs=[pl.BlockSpec((None, N_HEADS_IDX, HEAD_DIM), lambda b, pt: (b, 0, 0)),
                      pl.BlockSpec(wi.shape, lambda b, pt: (0, 0)),
                      pl.BlockSpec(ki_new.shape, lambda b, pt: (0, 0)),
                      pl.BlockSpec(memory_space=pl.ANY)],
            out_specs=pl.BlockSpec((None, rows_pad, PAGE_SIZE), lambda b, pt: (b, 0, 0)),
            scratch_shapes=[pltpu.VMEM((n_pages, HEAD_DIM, PAGE_SIZE), F32), pltpu.SemaphoreType.DMA(())]),
        out_shape=jax.ShapeDtypeStruct((batch, rows_pad, PAGE_SIZE), F32),
        compiler_params=pltpu.CompilerParams(dimension_semantics=("arbitrary",),
                                             vmem_limit_bytes=VMEM_LIMIT_BYTES),
        name="score_sample",
    )(page_table, qi3, wi, ki_new, cache_kiT)


def _select_sample_kernel(s_ref, o_ref, key_ref, cut_ref, *, n_sel):
    rows, width = s_ref.shape
    n_bits = (width - 1).bit_length()
    idx = lax.broadcasted_iota(I32, (rows, width), 1)
    key_ref[...] = _sort_key(s_ref[...])
    sel = _select_mask(key_ref, idx, n_sel, n_bits, cut_ref)
    o_ref[...] = jnp.where(sel, 0.0, NEG_INF)


def _select_sample(scores, n_sel):
    rows, width = scores.shape
    return pl.pallas_call(
        functools.partial(_select_sample_kernel, n_sel=n_sel),
        out_shape=jax.ShapeDtypeStruct((rows, width), F32),
        scratch_shapes=[pltpu.VMEM((rows, width), I32), pltpu.VMEM((rows, 1), I32)],
        compiler_params=pltpu.CompilerParams(vmem_limit_bytes=VMEM_LIMIT_BYTES),
        name="select_sample",
    )(scores)


def _attn_sample_kernel(pt_ref, q_ref, bias_ref, kn_ref, vn_ref, ck_ref, cv_ref, o_ref,
                        kbuf, vbuf, s_ref, sem):
    b = pl.program_id(0)
    n_pages = kbuf.shape[0]
    _fetch_pages(pt_ref, b, n_pages, ck_ref, kbuf, sem.at[0])
    _fetch_pages(pt_ref, b, n_pages, cv_ref, vbuf, sem.at[1])
    q = q_ref[...]
    lane_low = lax.broadcasted_iota(I32, (1, LANES), 1) < HEAD_DIM
    lane0 = lax.broadcasted_iota(I32, (1, PAGE_SIZE), 1) == 0
    kn = kn_ref[pl.ds(b, 1), :]
    vn = vn_ref[pl.ds(b, 1), :]
    s_new = jnp.sum(q.astype(F32) * kn.astype(BF16).astype(F32), axis=1, keepdims=True)
    s_ref[n_pages] = jnp.where(lane0, s_new, NEG_INF) + bias_ref[pl.ds(n_pages, 1), :]
    _wait_pages(n_pages, ck_ref, kbuf, sem.at[0])

    def qk(p, m):
        kt = kbuf[p].reshape(A_KV, PAGE_SIZE).astype(BF16)
        s = _dot(q, kt) + bias_ref[pl.ds(p, 1), :]
        s_ref[p] = s
        return jnp.maximum(m, s)

    m = lax.fori_loop(0, n_pages, qk, s_ref[n_pages])
    m = jnp.max(m, axis=1, keepdims=True)
    _wait_pages(n_pages, cv_ref, vbuf, sem.at[1])

    def pv(p, carry):
        acc, l = carry
        e = jnp.exp(s_ref[p] - m)
        vt = vbuf[p].reshape(A_KV, PAGE_SIZE).astype(BF16)
        return acc + _dot_nt(e.astype(BF16), vt), l + e

    e_new = jnp.exp(s_ref[n_pages] - m)
    acc0 = jnp.sum(e_new, axis=1, keepdims=True).astype(BF16).astype(F32) * vn.astype(BF16).astype(F32)
    acc, l = lax.fori_loop(0, n_pages, pv, (acc0, e_new))
    out = acc / jnp.sum(l, axis=1, keepdims=True)
    o_ref[...] = jnp.where(lane_low, out[:GROUP], out[GROUP:]).astype(o_ref.dtype)


def _attn_sample(page_table, q8, bias3, k_new, v_new, cache_kT, cache_vT):
    batch, n_pages = page_table.shape
    rows_pad = bias3.shape[1]
    return pl.pallas_call(
        _attn_sample_kernel,
        grid_spec=pltpu.PrefetchScalarGridSpec(
            num_scalar_prefetch=1,
            grid=(batch,),
            in_specs=[pl.BlockSpec((None, N_HEADS_A, LANES), lambda b, pt: (b, 0, 0)),
                      pl.BlockSpec((None, rows_pad, PAGE_SIZE), lambda b, pt: (b, 0, 0)),
                      pl.BlockSpec(k_new.shape, lambda b, pt: (0, 0)),
                      pl.BlockSpec(v_new.shape, lambda b, pt: (0, 0)),
                      pl.BlockSpec(memory_space=pl.ANY),
                      pl.BlockSpec(memory_space=pl.ANY)],
            out_specs=pl.BlockSpec((None, GROUP, LANES), lambda b, pt: (b, 0, 0)),
            scratch_shapes=[pltpu.VMEM((n_pages, N_KV_A, HEAD_DIM, PAGE_SIZE), F32),
                            pltpu.VMEM((n_pages, N_KV_A, HEAD_DIM, PAGE_SIZE), F32),
                            pltpu.VMEM((rows_pad, N_HEADS_A, PAGE_SIZE), F32),
                            pltpu.SemaphoreType.DMA((2,))]),
        out_shape=jax.ShapeDtypeStruct((batch, GROUP, LANES), F32),
        compiler_params=pltpu.CompilerParams(dimension_semantics=("arbitrary",),
                                             vmem_limit_bytes=VMEM_LIMIT_BYTES),
        name="attn_sample",
    )(page_table, q8, bias3, k_new, v_new, cache_kT, cache_vT)


def _hgrn_sample_kernel(rq_ref, rf_ref, ri_ref, rg_ref, lb_ref, ng_ref, s_ref, o_ref, s_out_ref):
    b = pl.program_id(0)
    eye = _eye(HEAD_DIM)
    row = lambda ref: ref[pl.ds(b, 1), :]
    lb = lb_ref[...]
    f = lb + (1.0 - lb) * jax.nn.sigmoid(row(rf_ref))
    q, k, v, g = row(rq_ref), 1.0 - f, row(ri_ref), row(rg_ref)
    gate = ng_ref[...] * (g * jax.nn.sigmoid(g))
    outs = []
    for h in range(N_HEADS_R):
        sl = slice(h * HEAD_DIM, (h + 1) * HEAD_DIM)
        s_new = _column(f[:, sl], eye) * s_ref[h] + _column(k[:, sl], eye) * v[:, sl]
        s_out_ref[h] = s_new
        o = jnp.sum(_column(q[:, sl], eye) * s_new, axis=0, keepdims=True)
        outs.append(o * lax.rsqrt(jnp.mean(o * o, axis=1, keepdims=True) + RMS_EPS))
    o_ref[pl.ds(b, 1), :] = jnp.concatenate(outs, axis=1) * gate


def _hgrn_sample(rq, rf, ri, rg, lb, norm_g, state):
    batch = rq.shape[0]
    full = lambda a: pl.BlockSpec(a.shape, lambda b, _nd=a.ndim: (0,) * _nd)
    st = pl.BlockSpec((None, N_HEADS_R, HEAD_DIM, HEAD_DIM), lambda b: (b, 0, 0, 0))
    lb2, ng2 = lb.reshape(1, -1), norm_g.reshape(1, -1)
    return pl.pallas_call(
        _hgrn_sample_kernel,
        grid=(batch,),
        in_specs=[full(rq), full(rf), full(ri), full(rg), full(lb2), full(ng2), st],
        out_specs=[full(rq), st],
        out_shape=[jax.ShapeDtypeStruct(rq.shape, F32), jax.ShapeDtypeStruct(state.shape, F32)],
        compiler_params=pltpu.CompilerParams(dimension_semantics=("arbitrary",)),
        name="hgrn_sample",
    )(rq, rf, ri, rg, lb2, ng2, state)


FFN_CHUNK = 256
ROW_TILE = 512
Q_TILE = 128
HGRN_TILE = 128


def kernel(x_prompt, x_sample, cache_k, cache_v, cache_kidx, state_hgrn, page_table, w_in, w_out, hgrn_lb,
           hgrn_norm_g, ffn1_wg, ffn1_wu, ffn1_wd, ffn2_wg, ffn2_wu, ffn2_wd, ln1_g, ln1_b, ln2_g, ln2_b,
           ln3_g, ln3_b):
    assert w_in.shape[0] == DEPTH == 1
    batch, seq, d_model = x_prompt.shape
    dec_batch, dec_seq, _ = x_sample.shape
    assert dec_seq == 1
    n_pages = page_table.shape[1]
    l = 0

    lb = jnp.cumsum(jax.nn.softmax(hgrn_lb.astype(F32), axis=0), axis=0)[l]
    w1 = _ffn_weights(ffn1_wg[l], ffn1_wu[l], ffn1_wd[l], FFN_CHUNK)
    w2 = _ffn_weights(ffn2_wg[l], ffn2_wu[l], ffn2_wd[l], FFN_CHUNK)
    w_cat = _proj_weight(w_in[l])
    wo = w_out[l].astype(BF16)
    wo_a, wo_r = wo[:A_Q][_q_perm()], wo[A_Q:]

    xp = x_prompt.reshape(batch * seq, d_model)
    hp = _ffn(xp, w1, ln1_g[l], ln1_b[l], ROW_TILE)
    (q, k, v, kb, vb, qi, ki, kk, wi, rq, rf, ri, rg) = _proj(hp, w_cat, ROW_TILE)
    oa = _attn_prompt(q, qi, wi, kk, kb, vb, batch, Q_TILE)
    orr, sp = _hgrn_prompt(rq, rf, ri, rg, lb, hgrn_norm_g[l], batch, HGRN_TILE)
    yp = _ffn(hp, w2, ln3_g[l], ln3_b[l], ROW_TILE, mix=(oa, orr, wo_a, wo_r, ln2_g[l], ln2_b[l]))

    xs = x_sample.reshape(dec_batch, d_model)
    hs = _ffn(xs, w1, ln1_g[l], ln1_b[l], dec_batch)
    (qs, ks, vs, _, _, qis, kis, _, wis, rqs, rfs, ris, rgs) = _proj(hs, w_cat, dec_batch)
    rows_pad = -(-(n_pages + 1) // SUBLANES) * SUBLANES
    cache_kiT = jnp.transpose(cache_kidx[l], (0, 2, 1))
    cache_kT = jnp.transpose(cache_k[l], (0, 2, 3, 1))
    cache_vT = jnp.transpose(cache_v[l], (0, 2, 3, 1))
    scores = _score_sample(page_table, qis.reshape(dec_batch, N_HEADS_IDX, HEAD_DIM), wis, kis, cache_kiT,
                           rows_pad)
    n_sel = min(TOPK_MAX, (n_pages * PAGE_SIZE + 1) // 4)
    bias = _select_sample(scores.reshape(dec_batch, rows_pad * PAGE_SIZE), n_sel)
    q_heads = qs.reshape(dec_batch, GROUP, 2, HEAD_DIM).transpose(0, 2, 1, 3).reshape(
        dec_batch, N_HEADS_A, HEAD_DIM)
    zeros = jnp.zeros_like(q_heads)
    q8 = jnp.concatenate([jnp.concatenate([q_heads[:, :GROUP], zeros[:, :GROUP]], -1),
                          jnp.concatenate([zeros[:, GROUP:], q_heads[:, GROUP:]], -1)], 1)
    oas = _attn_sample(page_table, q8, bias.reshape(dec_batch, rows_pad, PAGE_SIZE), ks, vs,
                       cache_kT, cache_vT).reshape(dec_batch, A_Q).astype(BF16)
    ors, ss = _hgrn_sample(rqs, rfs, ris, rgs, lb, hgrn_norm_g[l], state_hgrn[l])
    ys = _ffn(hs, w2, ln3_g[l], ln3_b[l], dec_batch,
              mix=(oas, ors.astype(BF16), wo_a, wo_r, ln2_g[l], ln2_b[l]))

    return (yp.reshape(batch, seq, d_model), ys.reshape(dec_batch, 1, d_model),
            k.reshape(1, batch, seq, N_KV_A, HEAD_DIM), v.reshape(1, batch, seq, N_KV_A, HEAD_DIM),
            ki.reshape(1, batch, seq, HEAD_DIM), sp[None],
            ks.reshape(1, dec_batch, 1, N_KV_A, HEAD_DIM), vs.reshape(1, dec_batch, 1, N_KV_A, HEAD_DIM),
            kis.reshape(1, dec_batch, 1, HEAD_DIM), ss[None])
```

```python
import functools

import numpy as np
import jax
import jax.numpy as jnp
from jax import lax
from jax.experimental import pallas as pl
from jax.experimental.pallas import tpu as pltpu

F32 = jnp.float32
BF16 = jnp.bfloat16
I32 = jnp.int32

DEPTH = 1
N_HEADS_A = 8
N_KV_A = 2
HEAD_DIM = 64
N_HEADS_IDX = 8
N_HEADS_R = 8
TOPK_MAX = 256
PAGE_SIZE = 128
LN_EPS = 1e-5
RMS_EPS = 1e-6
DN_ALPHA = (2 * DEPTH) ** 0.25

LANES = 128
SUBLANES = 8
VMEM_LIMIT_BYTES = 56 * 1024 * 1024

NEG_INF = float("-inf")
INT_MIN = -(2 ** 31)
_NEG_INF_BITS = int(np.float32(NEG_INF).view(np.int32))
NEG_INF_KEY = _NEG_INF_BITS ^ 0x7FFFFFFF


def _dot(a, b):
    return jnp.dot(a, b, preferred_element_type=F32)


def _dot_nt(a, b):
    return lax.dot_general(a, b, (((1,), (1,)), ((), ())), preferred_element_type=F32)


def _dot_tn(a, b):
    return lax.dot_general(a, b, (((0,), (0,)), ((), ())), preferred_element_type=F32)


def _split3(x):
    hi = x.astype(BF16)
    r1 = x - hi.astype(F32)
    mid = r1.astype(BF16)
    lo = (r1 - mid.astype(F32)).astype(BF16)
    return hi, mid, lo


def _dot_exact_lhs(m_bf, x):
    hi, mid, lo = _split3(x)
    return _dot(m_bf, hi) + _dot(m_bf, mid) + _dot(m_bf, lo)


def _dot_exact_rhs(x, m_bf):
    hi, mid, lo = _split3(x)
    return _dot(hi, m_bf) + _dot(mid, m_bf) + _dot(lo, m_bf)


def _layer_norm(x, g, b):
    mu = jnp.mean(x, -1, keepdims=True)
    xc = x - mu
    var = jnp.mean(xc * xc, -1, keepdims=True)
    return xc * lax.rsqrt(var + LN_EPS) * g + b


def _ffn_kernel(*refs, n_chunks, with_mix):
    if with_mix:
        (h_ref, oa_ref, or_ref, woa_ref, wor_ref, g2_ref, b2_ref,
         wg_ref, wu_ref, wd_ref, g_ref, b_ref, o_ref, acc_ref, x_ref) = refs
        m = _dot(oa_ref[...], woa_ref[...]) + _dot(or_ref[...], wor_ref[...])
        x_ref[...] = _layer_norm(DN_ALPHA * h_ref[...] + m, g2_ref[...], b2_ref[...])
    else:
        x_ref, wg_ref, wu_ref, wd_ref, g_ref, b_ref, o_ref, acc_ref = refs
    xb = x_ref[...].astype(BF16)
    acc_ref[...] = jnp.zeros_like(acc_ref)

    def body(c, carry):
        gg = _dot(xb, wg_ref[c])
        uu = _dot(xb, wu_ref[c])
        hh = (gg * jax.nn.sigmoid(gg) * uu).astype(BF16)
        acc_ref[...] += _dot(hh, wd_ref[c])
        return carry

    lax.fori_loop(0, n_chunks, body, 0)
    o_ref[...] = _layer_norm(DN_ALPHA * x_ref[...] + 0.5 * acc_ref[...], g_ref[...], b_ref[...])


def _const_spec(shape):
    nd = len(shape)
    return pl.BlockSpec(shape, lambda i, _nd=nd: (0,) * _nd, pipeline_mode=pl.Buffered(1))


def _ffn_weights(wg, wu, wd, fc):
    d, ff = wg.shape
    n = ff // fc
    wg3 = wg.astype(BF16).reshape(d, n, fc).transpose(1, 0, 2)
    wu3 = wu.astype(BF16).reshape(d, n, fc).transpose(1, 0, 2)
    wd3 = wd.astype(BF16).reshape(n, fc, d)
    return wg3, wu3, wd3


def _ffn(x, w3, g, b, tm, mix=None):
    wg3, wu3, wd3 = w3
    n_rows, d = x.shape
    n_chunks, _, fc = wg3.shape
    assert n_rows % tm == 0
    row = lambda w: pl.BlockSpec((tm, w), lambda i: (i, 0))
    vec = lambda a: a.reshape(1, -1).astype(F32)
    args, specs = [x], [row(d)]
    scratch = [pltpu.VMEM((tm, d), F32)]
    if mix is not None:
        oa, orr, woa, wor, g2, b2 = mix
        args += [oa, orr, woa, wor, vec(g2), vec(b2)]
        specs += [row(oa.shape[1]), row(orr.shape[1]), _const_spec(woa.shape), _const_spec(wor.shape),
                  _const_spec((1, d)), _const_spec((1, d))]
        scratch.append(pltpu.VMEM((tm, d), F32))
    args += [wg3, wu3, wd3, vec(g), vec(b)]
    specs += [_const_spec(wg3.shape), _const_spec(wu3.shape), _const_spec(wd3.shape),
              _const_spec((1, d)), _const_spec((1, d))]
    return pl.pallas_call(
        functools.partial(_ffn_kernel, n_chunks=n_chunks, with_mix=mix is not None),
        grid=(n_rows // tm,),
        in_specs=specs,
        out_specs=row(d),
        out_shape=jax.ShapeDtypeStruct((n_rows, d), F32),
        scratch_shapes=scratch,
        compiler_params=pltpu.CompilerParams(dimension_semantics=("arbitrary",),
                                             vmem_limit_bytes=VMEM_LIMIT_BYTES),
        name="ffn_mix" if mix is not None else "ffn",
    )(*args)


A_Q = N_HEADS_A * HEAD_DIM
A_KV = N_KV_A * HEAD_DIM
I_Q = N_HEADS_IDX * HEAD_DIM
R_W = N_HEADS_R * HEAD_DIM
_SPLIT = (A_Q, A_KV, A_KV, I_Q, HEAD_DIM, N_HEADS_IDX, R_W, R_W, R_W, R_W)
_OFF = tuple(int(v) for v in np.cumsum((0,) + _SPLIT))
GROUP = N_HEADS_A // N_KV_A


def _q_perm():
    idx = []
    for j in range(GROUP):
        idx += list(range(j * HEAD_DIM, (j + 1) * HEAD_DIM))
        idx += list(range((GROUP + j) * HEAD_DIM, (GROUP + j + 1) * HEAD_DIM))
    return np.asarray(idx, np.int32)


def _proj_weight(w_in):
    seg = lambda i: w_in[:, _OFF[i]:_OFF[i + 1]]
    q = seg(0)[:, _q_perm()]
    ki = seg(4)
    wi = jnp.pad(seg(5), ((0, 0), (0, LANES - N_HEADS_IDX)))
    cols = [q, seg(1), seg(2), seg(3), ki, ki, wi, seg(6), seg(7), seg(8), seg(9)]
    return jnp.concatenate(cols, axis=1).astype(BF16)


def _proj_kernel(h_ref, w_ref, q_ref, k_ref, v_ref, kb_ref, vb_ref, qi_ref, ki_ref, kk_ref, wi_ref,
                 rq_ref, rf_ref, ri_ref, rg_ref):
    hb = h_ref[...].astype(BF16)
    col = [0]

    def take(width):
        out = _dot(hb, w_ref[:, col[0]:col[0] + width])
        col[0] += width
        return out

    scale = HEAD_DIM ** -0.5
    q_ref[...] = (take(A_Q) * scale).astype(BF16)
    k = take(A_KV)
    k_ref[...] = k
    kb_ref[...] = k.astype(BF16)
    v = take(A_KV)
    v_ref[...] = v
    vb_ref[...] = v.astype(BF16)
    qi_ref[...] = (take(I_Q) * scale).astype(BF16)
    kk = take(LANES)
    ki_ref[...] = kk[:, :HEAD_DIM]
    kk_ref[...] = kk.astype(BF16)
    wi_ref[...] = take(LANES) * (N_HEADS_IDX ** -0.5)
    rq_ref[...] = take(R_W)
    rf_ref[...] = take(R_W)
    ri_ref[...] = take(R_W)
    rg_ref[...] = take(R_W)


def _proj(h, w_cat, tm):
    n_rows, d = h.shape
    assert n_rows % tm == 0
    row = lambda w: pl.BlockSpec((tm, w), lambda i: (i, 0))
    outs = [(A_Q, BF16), (A_KV, F32), (A_KV, F32), (A_KV, BF16), (A_KV, BF16), (I_Q, BF16),
            (HEAD_DIM, F32), (LANES, BF16), (LANES, F32), (R_W, F32), (R_W, F32), (R_W, F32), (R_W, F32)]
    return pl.pallas_call(
        _proj_kernel,
        grid=(n_rows // tm,),
        in_specs=[row(d), _const_spec(w_cat.shape)],
        out_specs=[row(w) for w, _ in outs],
        out_shape=[jax.ShapeDtypeStruct((n_rows, w), dt) for w, dt in outs],
        compiler_params=pltpu.CompilerParams(dimension_semantics=("arbitrary",),
                                             vmem_limit_bytes=VMEM_LIMIT_BYTES),
        name="in_proj",
    )(h, w_cat)


def _sort_key(score):
    bits = pltpu.bitcast(score + 0.0, I32)
    return bits ^ ((bits >> 31) & jnp.int32(0x7FFFFFFF))


def _count(mask):
    return jnp.sum(jnp.where(mask, 1.0, 0.0), axis=1, keepdims=True)


def _kth_largest_key(key_ref, n_sel):
    rows = key_ref.shape[0]

    def body(i, t_u):
        bit = jnp.left_shift(jnp.int32(1), 31 - i)
        cand_u = t_u | bit
        cnt = _count(key_ref[...] >= (cand_u ^ jnp.int32(INT_MIN)))
        return jnp.where(cnt >= n_sel, cand_u, t_u)

    t_u = lax.fori_loop(0, 32, body, jnp.zeros((rows, 1), I32))
    return t_u ^ jnp.int32(INT_MIN)


def _tie_cut(key_ref, thr, idx, n_sel, n_bits):
    rows, width = key_ref.shape
    need = n_sel - _count(key_ref[...] > thr)

    def body(i, j):
        cand = j | jnp.left_shift(jnp.int32(1), n_bits - 1 - i)
        cnt = _count((key_ref[...] == thr) & (idx < cand))
        return jnp.where(cnt <= need, cand, j)

    return lax.fori_loop(0, n_bits, body, jnp.zeros((rows, 1), I32))


def _select_mask(key_ref, idx, n_sel, n_bits, cut_ref):
    thr = _kth_largest_key(key_ref, n_sel)
    keys = key_ref[...]
    eq = keys == thr
    excess = ((_count(keys > thr) + _count(eq)) > n_sel) & (thr != NEG_INF_KEY)
    cut_ref[...] = jnp.full(cut_ref.shape, 1 << n_bits, I32)

    @pl.when(jnp.max(jnp.where(excess, 1.0, 0.0)) > 0.0)
    def _():
        cut_ref[...] = _tie_cut(key_ref, thr, idx, n_sel, n_bits + 1)

    return (keys > thr) | (eq & (idx < cut_ref[...]))


def _attn_prompt_kernel(q_ref, qi_ref, wi_ref, kk_ref, k_ref, v_ref, o_ref, key_ref, cut_ref, *, n_sel, kv_step):
    tq = q_ref.shape[0]
    s_len = k_ref.shape[0]
    last = (pl.program_id(1) + 1) * tq - 1
    for n_kv in range(kv_step, s_len + 1, kv_step):
        @pl.when((last >= n_kv - kv_step) & (last < n_kv))
        def _(n_kv=n_kv):
            _attn_prompt_tile(q_ref, qi_ref, wi_ref, kk_ref.at[:n_kv], k_ref.at[:n_kv], v_ref.at[:n_kv], o_ref,
                              key_ref.at[:, :n_kv], cut_ref, n_sel)


def _attn_prompt_tile(q_ref, qi_ref, wi_ref, kk_ref, k_ref, v_ref, o_ref, key_ref, cut_ref, n_sel):
    tq = q_ref.shape[0]
    s_len = k_ref.shape[0]
    n_bits = (s_len - 1).bit_length()
    q_pos = pl.program_id(1) * tq + lax.broadcasted_iota(I32, (tq, 1), 0)
    key_pos = lax.broadcasted_iota(I32, (tq, s_len), 1)
    causal = key_pos <= q_pos
    low = lax.broadcasted_iota(I32, (1, LANES), 1) < HEAD_DIM
    zero = jnp.zeros((), BF16)

    kk = kk_ref[...]
    score = jnp.zeros((tq, s_len), F32)
    for j in range(I_Q // LANES):
        blk = qi_ref[:, j * LANES:(j + 1) * LANES]
        for half in range(2):
            h = 2 * j + half
            qm = jnp.where(low if half == 0 else ~low, blk, zero)
            score = score + wi_ref[:, h:h + 1] * jnp.maximum(_dot_nt(qm, kk), 0.0)
    key_ref[...] = _sort_key(jnp.where(causal, score, NEG_INF))
    sel = _select_mask(key_ref, key_pos, n_sel, n_bits, cut_ref) & causal
    bias = jnp.where(sel, 0.0, NEG_INF)

    kb = k_ref[...]
    vb = v_ref[...]
    for j in range(GROUP):
        blk = q_ref[:, j * LANES:(j + 1) * LANES]
        outs = []
        for half in range(2):
            qm = jnp.where(low if half == 0 else ~low, blk, zero)
            s = _dot_nt(qm, kb) + bias
            m = jnp.max(s, axis=1, keepdims=True)
            p = jnp.exp(s - m)
            l = jnp.sum(p, axis=1, keepdims=True)
            outs.append(_dot(p.astype(BF16), vb) / l)
        o_ref[:, j * LANES:(j + 1) * LANES] = jnp.where(low, outs[0], outs[1]).astype(o_ref.dtype)


def _attn_prompt(q, qi, wi, kk, kb, vb, batch, tq):
    n_rows = q.shape[0]
    s_len = n_rows // batch
    assert s_len & (s_len - 1) == 0 and s_len % tq == 0
    n_sel = min(TOPK_MAX, s_len // 4)
    nq = s_len // tq
    kv_step = min(KV_STEP, s_len)
    assert kv_step >= n_sel and kv_step % tq == 0 and s_len % kv_step == 0
    tile = lambda w: pl.BlockSpec((tq, w), lambda b, i: (b * nq + i, 0))
    full = lambda w: pl.BlockSpec((s_len, w), lambda b, i: (b, 0))
    return pl.pallas_call(
        functools.partial(_attn_prompt_kernel, n_sel=n_sel, kv_step=kv_step),
        grid=(batch, nq),
        in_specs=[tile(A_Q), tile(I_Q), tile(LANES), full(LANES), full(A_KV), full(A_KV)],
        out_specs=tile(A_Q),
        out_shape=jax.ShapeDtypeStruct((n_rows, A_Q), BF16),
        scratch_shapes=[pltpu.VMEM((tq, s_len), I32), pltpu.VMEM((tq, 1), I32)],
        compiler_params=pltpu.CompilerParams(dimension_semantics=("arbitrary", "arbitrary"),
                                             vmem_limit_bytes=VMEM_LIMIT_BYTES),
        name="attn_prompt",
    )(q, qi, wi, kk, kb, vb)


def _hgrn_consts(c):
    t = np.arange(c)
    tri = (t[None, :] <= t[:, None]).astype(np.float32)
    sels, masks = [], []
    hs = c // 2
    while hs >= 1:
        blk = t // (2 * hs)
        ref_row = blk * 2 * hs + hs - 1
        sels.append((t[None, :] == ref_row[:, None]).astype(np.float32))
        right = (t % (2 * hs)) >= hs
        masks.append(((blk[:, None] == blk[None, :]) & right[:, None] & ~right[None, :]).astype(np.float32))
        hs //= 2
    return (jnp.asarray(tri, BF16), jnp.asarray(np.concatenate(sels, 0), BF16),
            jnp.asarray(np.stack(masks, 0), F32))


def _head_block_ones():
    d = np.arange(LANES) // HEAD_DIM
    return (d[:, None] == d[None, :]).astype(np.float32)


def _hgrn_prompt_kernel(rq_ref, rf_ref, ri_ref, rg_ref, lb_ref, ng_ref, tri_ref, sel_ref, msk_ref, bo_ref,
                        o_ref, s_out_ref, st_ref):
    c = rq_ref.shape[0]
    n_levels = msk_ref.shape[0]
    step = pl.program_id(2)

    @pl.when(step == 0)
    def _():
        st_ref[...] = jnp.zeros_like(st_ref)

    lb = lb_ref[...]
    f = lb + (1.0 - lb) * jax.nn.sigmoid(rf_ref[...])
    logf = jnp.log(f)
    q = rq_ref[...]
    k = 1.0 - f
    v = ri_ref[...]
    vb = v.astype(BF16)
    bo = bo_ref[...]
    same_head = bo > 0.5
    low = lax.broadcasted_iota(I32, (1, LANES), 1) < HEAD_DIM

    b = _dot_exact_lhs(tri_ref[...], logf)
    st = st_ref[...]
    qd = (q * jnp.exp(b)).astype(BF16)
    o = _dot_nt(qd, st.astype(BF16))
    o = o + _dot((q * k).astype(BF16), bo.astype(BF16)) * v

    a0 = jnp.zeros((c, c), F32)
    a1 = jnp.zeros((c, c), F32)
    hi, mid, lo = _split3(b)
    for lv in range(n_levels):
        sel = sel_ref[lv * c:(lv + 1) * c, :]
        e = _dot(sel, hi) + _dot(sel, mid) + _dot(sel, lo)
        diff = b - e
        qdl = q * jnp.exp(jnp.minimum(diff, 0.0))
        kdl = (k * jnp.exp(jnp.minimum(-diff, 0.0))).astype(BF16)
        msk = msk_ref[lv] > 0.5
        a0 = a0 + jnp.where(msk, _dot_nt(jnp.where(low, qdl, 0.0).astype(BF16), kdl), 0.0)
        a1 = a1 + jnp.where(msk, _dot_nt(jnp.where(low, 0.0, qdl).astype(BF16), kdl), 0.0)
    o = o + jnp.where(low, _dot(a0.astype(BF16), vb), _dot(a1.astype(BF16), vb))

    b_last = b[c - 1:c, :]
    kdec = (k * jnp.exp(jnp.minimum(b_last - b, 0.0))).astype(BF16)
    st_new = st * jnp.exp(b_last) + jnp.where(same_head, _dot_tn(vb, kdec), 0.0)
    st_ref[...] = st_new

    ms = _dot_exact_rhs(o * o, bo.astype(BF16)) * (1.0 / HEAD_DIM)
    g = rg_ref[...]
    o_ref[...] = (o * lax.rsqrt(ms + RMS_EPS) * ng_ref[...] * (g * jax.nn.sigmoid(g))).astype(o_ref.dtype)

    @pl.when(step == pl.num_programs(2) - 1)
    def _():
        s_out_ref[...] = st_new.T


def _hgrn_prompt(rq, rf, ri, rg, lb, norm_g, batch, chunk):
    n_rows = rq.shape[0]
    t_len = n_rows // batch
    assert t_len % chunk == 0 and chunk & (chunk - 1) == 0
    nc = t_len // chunk
    n_pairs = R_W // LANES
    tri, sel, msk = _hgrn_consts(chunk)
    bo = jnp.asarray(_head_block_ones(), F32)
    tile = pl.BlockSpec((chunk, LANES), lambda b, p, i: (b * nc + i, p))
    vec = pl.BlockSpec((1, LANES), lambda b, p, i: (0, p))
    const = lambda a: pl.BlockSpec(a.shape, lambda b, p, i, _nd=a.ndim: (0,) * _nd)
    o, s_pairs = pl.pallas_call(
        _hgrn_prompt_kernel,
        grid=(batch, n_pairs, nc),
        in_specs=[tile, tile, tile, tile, vec, vec, const(tri), const(sel), const(msk), const(bo)],
        out_specs=[tile, pl.BlockSpec((None, None, LANES, LANES), lambda b, p, i: (b, p, 0, 0))],
        out_shape=[jax.ShapeDtypeStruct((n_rows, R_W), BF16),
                   jax.ShapeDtypeStruct((batch, n_pairs, LANES, LANES), F32)],
        scratch_shapes=[pltpu.VMEM((LANES, LANES), F32)],
        compiler_params=pltpu.CompilerParams(dimension_semantics=("arbitrary", "arbitrary", "arbitrary"),
                                             vmem_limit_bytes=VMEM_LIMIT_BYTES),
        name="hgrn_prompt",
    )(rq, rf, ri, rg, lb.reshape(1, -1), norm_g.reshape(1, -1), tri, sel, msk, bo)
    s0 = s_pairs[:, :, :HEAD_DIM, :HEAD_DIM]
    s1 = s_pairs[:, :, HEAD_DIM:, HEAD_DIM:]
    return o, jnp.stack([s0, s1], axis=2).reshape(batch, N_HEADS_R, HEAD_DIM, HEAD_DIM)


def _column(row, eye):
    return jnp.sum(jnp.where(eye, row, 0.0), axis=1, keepdims=True)


def _eye(n):
    return lax.broadcasted_iota(I32, (n, n), 0) == lax.broadcasted_iota(I32, (n, n), 1)


PAGE_UNROLL = 8


def _fetch_pages(pt_ref, b, n_pages, src_ref, dst_ref, sem):
    def start(p, carry):
        pltpu.make_async_copy(src_ref.at[pt_ref[b, p]], dst_ref.at[p], sem).start()
        return carry

    lax.fori_loop(0, n_pages, start, 0, unroll=PAGE_UNROLL)


def _wait_pages(n_pages, src_ref, dst_ref, sem):
    def wait(p, carry):
        pltpu.make_async_copy(src_ref.at[0], dst_ref.at[p], sem).wait()
        return carry

    lax.fori_loop(0, n_pages, wait, 0, unroll=PAGE_UNROLL)


def _prefetch_sequence(pt_ref, n_pages, streams):
    b = pl.program_id(0)
    slot = lax.rem(b, 2)

    @pl.when(b == 0)
    def _():
        for src, buf, sem in streams:
            _fetch_pages(pt_ref, 0, n_pages, src, buf.at[0], sem.at[0])

    @pl.when(b + 1 < pl.num_programs(0))
    def _():
        for src, buf, sem in streams:
            _fetch_pages(pt_ref, b + 1, n_pages, src, buf.at[1 - slot], sem.at[1 - slot])

    return slot


def _score_sample_kernel(pt_ref, qi_ref, wi_ref, kin_ref, cache_ref, o_ref, kbuf2, sem):
    b = pl.program_id(0)
    n_pages = kbuf2.shape[1]
    slot = _prefetch_sequence(pt_ref, n_pages, [(cache_ref, kbuf2, sem)])
    kbuf = kbuf2.at[slot]
    qi = qi_ref[...]
    wcol = _column(wi_ref[pl.ds(b, 1), :][:, :N_HEADS_IDX], _eye(N_HEADS_IDX))
    lane0 = lax.broadcasted_iota(I32, (1, PAGE_SIZE), 1) == 0
    ki_new = kin_ref[pl.ds(b, 1), :].astype(BF16).astype(F32)
    lg_new = jnp.sum(qi.astype(F32) * ki_new, axis=1, keepdims=True)
    sc_new = jnp.sum(wcol * jnp.maximum(lg_new, 0.0), axis=0, keepdims=True)
    o_ref[...] = jnp.full(o_ref.shape, NEG_INF, F32)
    o_ref[pl.ds(n_pages, 1), :] = jnp.where(lane0, sc_new, NEG_INF)
    _wait_pages(n_pages, cache_ref, kbuf, sem.at[slot])

    def page(p, carry):
        lg = _dot(qi, kbuf[p].astype(BF16))
        o_ref[pl.ds(p, 1), :] = jnp.sum(wcol * jnp.maximum(lg, 0.0), axis=0, keepdims=True)
        return carry

    lax.fori_loop(0, n_pages, page, 0, unroll=PAGE_UNROLL)


def _score_sample(page_table, qi3, wi, ki_new, cache_kiT, rows_pad):
    batch, n_pages = page_table.shape
    return pl.pallas_call(
        _score_sample_kernel,
        grid_spec=pltpu.PrefetchScalarGridSpec(
            num_scalar_prefetch=1,
            grid=(batch,),
            in_specs=[pl.BlockSpec((None, N_HEADS_IDX, HEAD_DIM), lambda b, pt: (b, 0, 0)),
                      pl.BlockSpec(wi.shape, lambda b, pt: (0, 0)),
                      pl.BlockSpec(ki_new.shape, lambda b, pt: (0, 0)),
                      pl.BlockSpec(memory_space=pl.ANY)],
            out_specs=pl.BlockSpec((None, rows_pad, PAGE_SIZE), lambda b, pt: (b, 0, 0)),
            scratch_shapes=[pltpu.VMEM((2, n_pages, HEAD_DIM, PAGE_SIZE), F32), pltpu.SemaphoreType.DMA((2,))]),
        out_shape=jax.ShapeDtypeStruct((batch, rows_pad, PAGE_SIZE), F32),
        compiler_params=pltpu.CompilerParams(dimension_semantics=("arbitrary",),
                                             vmem_limit_bytes=VMEM_LIMIT_BYTES),
        name="score_sample",
    )(page_table, qi3, wi, ki_new, cache_kiT)


def _select_sample_kernel(s_ref, o_ref, key_ref, cut_ref, *, n_sel):
    rows, width = s_ref.shape
    n_bits = (width - 1).bit_length()
    idx = lax.broadcasted_iota(I32, (rows, width), 1)
    key_ref[...] = _sort_key(s_ref[...])
    sel = _select_mask(key_ref, idx, n_sel, n_bits, cut_ref)
    o_ref[...] = jnp.where(sel, 0.0, NEG_INF)


def _select_sample(scores, n_sel):
    rows, width = scores.shape
    return pl.pallas_call(
        functools.partial(_select_sample_kernel, n_sel=n_sel),
        out_shape=jax.ShapeDtypeStruct((rows, width), F32),
        scratch_shapes=[pltpu.VMEM((rows, width), I32), pltpu.VMEM((rows, 1), I32)],
        compiler_params=pltpu.CompilerParams(vmem_limit_bytes=VMEM_LIMIT_BYTES),
        name="select_sample",
    )(scores)


def _attn_sample_kernel(pt_ref, q_ref, bias_ref, kn_ref, vn_ref, ck_ref, cv_ref, o_ref,
                        kbuf2, vbuf2, s_ref, ksem, vsem):
    b = pl.program_id(0)
    n_pages = kbuf2.shape[1]
    slot = _prefetch_sequence(pt_ref, n_pages, [(ck_ref, kbuf2, ksem), (cv_ref, vbuf2, vsem)])
    kbuf, vbuf = kbuf2.at[slot], vbuf2.at[slot]
    q = q_ref[...]
    lane_low = lax.broadcasted_iota(I32, (1, LANES), 1) < HEAD_DIM
    lane0 = lax.broadcasted_iota(I32, (1, PAGE_SIZE), 1) == 0
    kn = kn_ref[pl.ds(b, 1), :]
    vn = vn_ref[pl.ds(b, 1), :]
    s_new = jnp.sum(q.astype(F32) * kn.astype(BF16).astype(F32), axis=1, keepdims=True)
    s_ref[n_pages] = jnp.where(lane0, s_new, NEG_INF) + bias_ref[pl.ds(n_pages, 1), :]
    _wait_pages(n_pages, ck_ref, kbuf, ksem.at[slot])

    def qk(p, m):
        kt = kbuf[p].reshape(A_KV, PAGE_SIZE).astype(BF16)
        s = _dot(q, kt) + bias_ref[pl.ds(p, 1), :]
        s_ref[p] = s
        return jnp.maximum(m, s)

    m = lax.fori_loop(0, n_pages, qk, s_ref[n_pages], unroll=PAGE_UNROLL)
    m = jnp.max(m, axis=1, keepdims=True)
    _wait_pages(n_pages, cv_ref, vbuf, vsem.at[slot])

    def pv(p, carry):
        acc, l = carry
        e = jnp.exp(s_ref[p] - m)
        vt = vbuf[p].reshape(A_KV, PAGE_SIZE).astype(BF16)
        return acc + _dot_nt(e.astype(BF16), vt), l + e

    e_new = jnp.exp(s_ref[n_pages] - m)
    acc0 = jnp.sum(e_new, axis=1, keepdims=True).astype(BF16).astype(F32) * vn.astype(BF16).astype(F32)
    acc, l = lax.fori_loop(0, n_pages, pv, (acc0, e_new), unroll=PAGE_UNROLL)
    out = acc / jnp.sum(l, axis=1, keepdims=True)
    o_ref[...] = jnp.where(lane_low, out[:GROUP], out[GROUP:]).astype(o_ref.dtype)


def _attn_sample(page_table, q8, bias3, k_new, v_new, cache_kT, cache_vT):
    batch, n_pages = page_table.shape
    rows_pad = bias3.shape[1]
    return pl.pallas_call(
        _attn_sample_kernel,
        grid_spec=pltpu.PrefetchScalarGridSpec(
            num_scalar_prefetch=1,
            grid=(batch,),
            in_specs=[pl.BlockSpec((None, N_HEADS_A, LANES), lambda b, pt: (b, 0, 0)),
                      pl.BlockSpec((None, rows_pad, PAGE_SIZE), lambda b, pt: (b, 0, 0)),
                      pl.BlockSpec(k_new.shape, lambda b, pt: (0, 0)),
                      pl.BlockSpec(v_new.shape, lambda b, pt: (0, 0)),
                      pl.BlockSpec(memory_space=pl.ANY),
                      pl.BlockSpec(memory_space=pl.ANY)],
            out_specs=pl.BlockSpec((None, GROUP, LANES), lambda b, pt: (b, 0, 0)),
            scratch_shapes=[pltpu.VMEM((2, n_pages, N_KV_A, HEAD_DIM, PAGE_SIZE), F32),
                            pltpu.VMEM((2, n_pages, N_KV_A, HEAD_DIM, PAGE_SIZE), F32),
                            pltpu.VMEM((rows_pad, N_HEADS_A, PAGE_SIZE), F32),
                            pltpu.SemaphoreType.DMA((2,)), pltpu.SemaphoreType.DMA((2,))]),
        out_shape=jax.ShapeDtypeStruct((batch, GROUP, LANES), F32),
        compiler_params=pltpu.CompilerParams(dimension_semantics=("arbitrary",),
                                             vmem_limit_bytes=VMEM_LIMIT_BYTES),
        name="attn_sample",
    )(page_table, q8, bias3, k_new, v_new, cache_kT, cache_vT)


def _hgrn_sample_kernel(rq_ref, rf_ref, ri_ref, rg_ref, lb_ref, ng_ref, s_ref, o_ref, s_out_ref):
    b = pl.program_id(0)
    eye = _eye(HEAD_DIM)
    row = lambda ref: ref[pl.ds(b, 1), :]
    lb = lb_ref[...]
    f = lb + (1.0 - lb) * jax.nn.sigmoid(row(rf_ref))
    q, k, v, g = row(rq_ref), 1.0 - f, row(ri_ref), row(rg_ref)
    gate = ng_ref[...] * (g * jax.nn.sigmoid(g))
    outs = []
    for h in range(N_HEADS_R):
        sl = slice(h * HEAD_DIM, (h + 1) * HEAD_DIM)
        s_new = _column(f[:, sl], eye) * s_ref[h] + _column(k[:, sl], eye) * v[:, sl]
        s_out_ref[h] = s_new
        o = jnp.sum(_column(q[:, sl], eye) * s_new, axis=0, keepdims=True)
        outs.append(o * lax.rsqrt(jnp.mean(o * o, axis=1, keepdims=True) + RMS_EPS))
    o_ref[pl.ds(b, 1), :] = jnp.concatenate(outs, axis=1) * gate


def _hgrn_sample(rq, rf, ri, rg, lb, norm_g, state):
    batch = rq.shape[0]
    full = lambda a: pl.BlockSpec(a.shape, lambda b, _nd=a.ndim: (0,) * _nd)
    st = pl.BlockSpec((None, N_HEADS_R, HEAD_DIM, HEAD_DIM), lambda b: (b, 0, 0, 0))
    lb2, ng2 = lb.reshape(1, -1), norm_g.reshape(1, -1)
    return pl.pallas_call(
        _hgrn_sample_kernel,
        grid=(batch,),
        in_specs=[full(rq), full(rf), full(ri), full(rg), full(lb2), full(ng2), st],
        out_specs=[full(rq), st],
        out_shape=[jax.ShapeDtypeStruct(rq.shape, F32), jax.ShapeDtypeStruct(state.shape, F32)],
        compiler_params=pltpu.CompilerParams(dimension_semantics=("arbitrary",)),
        name="hgrn_sample",
    )(rq, rf, ri, rg, lb2, ng2, state)


FFN_CHUNK = 256
ROW_TILE = 512
Q_TILE = 128
KV_STEP = 512
HGRN_TILE = 128


def kernel(x_prompt, x_sample, cache_k, cache_v, cache_kidx, state_hgrn, page_table, w_in, w_out, hgrn_lb,
           hgrn_norm_g, ffn1_wg, ffn1_wu, ffn1_wd, ffn2_wg, ffn2_wu, ffn2_wd, ln1_g, ln1_b, ln2_g, ln2_b,
           ln3_g, ln3_b):
    assert w_in.shape[0] == DEPTH == 1
    batch, seq, d_model = x_prompt.shape
    dec_batch, dec_seq, _ = x_sample.shape
    assert dec_seq == 1
    n_pages = page_table.shape[1]
    l = 0

    lb = jnp.cumsum(jax.nn.softmax(hgrn_lb.astype(F32), axis=0), axis=0)[l]
    w1 = _ffn_weights(ffn1_wg[l], ffn1_wu[l], ffn1_wd[l], FFN_CHUNK)
    w2 = _ffn_weights(ffn2_wg[l], ffn2_wu[l], ffn2_wd[l], FFN_CHUNK)
    w_cat = _proj_weight(w_in[l])
    wo = w_out[l].astype(BF16)
    wo_a, wo_r = wo[:A_Q][_q_perm()], wo[A_Q:]

    xp = x_prompt.reshape(batch * seq, d_model)
    hp = _ffn(xp, w1, ln1_g[l], ln1_b[l], ROW_TILE)
    (q, k, v, kb, vb, qi, ki, kk, wi, rq, rf, ri, rg) = _proj(hp, w_cat, ROW_TILE)
    oa = _attn_prompt(q, qi, wi, kk, kb, vb, batch, Q_TILE)
    orr, sp = _hgrn_prompt(rq, rf, ri, rg, lb, hgrn_norm_g[l], batch, HGRN_TILE)
    yp = _ffn(hp, w2, ln3_g[l], ln3_b[l], ROW_TILE, mix=(oa, orr, wo_a, wo_r, ln2_g[l], ln2_b[l]))

    xs = x_sample.reshape(dec_batch, d_model)
    hs = _ffn(xs, w1, ln1_g[l], ln1_b[l], dec_batch)
    (qs, ks, vs, _, _, qis, kis, _, wis, rqs, rfs, ris, rgs) = _proj(hs, w_cat, dec_batch)
    rows_pad = -(-(n_pages + 1) // SUBLANES) * SUBLANES
    cache_kiT = jnp.transpose(cache_kidx[l], (0, 2, 1))
    cache_kT = jnp.transpose(cache_k[l], (0, 2, 3, 1))
    cache_vT = jnp.transpose(cache_v[l], (0, 2, 3, 1))
    scores = _score_sample(page_table, qis.reshape(dec_batch, N_HEADS_IDX, HEAD_DIM), wis, kis, cache_kiT,
                           rows_pad)
    n_sel = min(TOPK_MAX, (n_pages * PAGE_SIZE + 1) // 4)
    bias = _select_sample(scores.reshape(dec_batch, rows_pad * PAGE_SIZE), n_sel)
    q_heads = qs.reshape(dec_batch, GROUP, 2, HEAD_DIM).transpose(0, 2, 1, 3).reshape(
        dec_batch, N_HEADS_A, HEAD_DIM)
    zeros = jnp.zeros_like(q_heads)
    q8 = jnp.concatenate([jnp.concatenate([q_heads[:, :GROUP], zeros[:, :GROUP]], -1),
                          jnp.concatenate([zeros[:, GROUP:], q_heads[:, GROUP:]], -1)], 1)
    oas = _attn_sample(page_table, q8, bias.reshape(dec_batch, rows_pad, PAGE_SIZE), ks, vs,
                       cache_kT, cache_vT).reshape(dec_batch, A_Q).astype(BF16)
    ors, ss = _hgrn_sample(rqs, rfs, ris, rgs, lb, hgrn_norm_g[l], state_hgrn[l])
    ys = _ffn(hs, w2, ln3_g[l], ln3_b[l], dec_batch,
              mix=(oas, ors.astype(BF16), wo_a, wo_r, ln2_g[l], ln2_b[l]))

    return (yp.reshape(batch, seq, d_model), ys.reshape(dec_batch, 1, d_model),
            k.reshape(1, batch, seq, N_KV_A, HEAD_DIM), v.reshape(1, batch, seq, N_KV_A, HEAD_DIM),
            ki.reshape(1, batch, seq, HEAD_DIM), sp[None],
            ks.reshape(1, dec_batch, 1, N_KV_A, HEAD_DIM), vs.reshape(1, dec_batch, 1, N_KV_A, HEAD_DIM),
            kis.reshape(1, dec_batch, 1, HEAD_DIM), ss[None])
```

```python
import functools

import numpy as np
import jax
import jax.numpy as jnp
from jax import lax
from jax.experimental import pallas as pl
from jax.experimental.pallas import tpu as pltpu

F32 = jnp.float32
BF16 = jnp.bfloat16
I32 = jnp.int32

DEPTH = 1
N_HEADS_A = 8
N_KV_A = 2
HEAD_DIM = 64
N_HEADS_IDX = 8
N_HEADS_R = 8
TOPK_MAX = 256
PAGE_SIZE = 128
LN_EPS = 1e-5
RMS_EPS = 1e-6
DN_ALPHA = (2 * DEPTH) ** 0.25

LANES = 128
SUBLANES = 8
VMEM_LIMIT_BYTES = 56 * 1024 * 1024

NEG_INF = float("-inf")
INT_MIN = -(2 ** 31)


def _dot(a, b):
    return jnp.dot(a, b, preferred_element_type=F32)


def _dot_nt(a, b):
    return lax.dot_general(a, b, (((1,), (1,)), ((), ())), preferred_element_type=F32)


def _dot_tn(a, b):
    return lax.dot_general(a, b, (((0,), (0,)), ((), ())), preferred_element_type=F32)


def _split3(x):
    hi = x.astype(BF16)
    r1 = x - hi.astype(F32)
    mid = r1.astype(BF16)
    lo = (r1 - mid.astype(F32)).astype(BF16)
    return hi, mid, lo


def _dot_exact_lhs(m_bf, x):
    hi, mid, lo = _split3(x)
    return _dot(m_bf, hi) + _dot(m_bf, mid) + _dot(m_bf, lo)


def _dot_exact_rhs(x, m_bf):
    hi, mid, lo = _split3(x)
    return _dot(hi, m_bf) + _dot(mid, m_bf) + _dot(lo, m_bf)


def _layer_norm(x, g, b):
    mu = jnp.mean(x, -1, keepdims=True)
    xc = x - mu
    var = jnp.mean(xc * xc, -1, keepdims=True)
    return xc * lax.rsqrt(var + LN_EPS) * g + b


def _ffn_kernel(*refs, n_chunks, with_mix):
    if with_mix:
        (h_ref, oa_ref, or_ref, woa_ref, wor_ref, g2_ref, b2_ref,
         wg_ref, wu_ref, wd_ref, g_ref, b_ref, o_ref, acc_ref, x_ref) = refs
        m = _dot(oa_ref[...], woa_ref[...]) + _dot(or_ref[...], wor_ref[...])
        x_ref[...] = _layer_norm(DN_ALPHA * h_ref[...] + m, g2_ref[...], b2_ref[...])
    else:
        x_ref, wg_ref, wu_ref, wd_ref, g_ref, b_ref, o_ref, acc_ref = refs
    xb = x_ref[...].astype(BF16)
    acc_ref[...] = jnp.zeros_like(acc_ref)

    def body(c, carry):
        gg = _dot(xb, wg_ref[c])
        uu = _dot(xb, wu_ref[c])
        hh = (gg * jax.nn.sigmoid(gg) * uu).astype(BF16)
        acc_ref[...] += _dot(hh, wd_ref[c])
        return carry

    lax.fori_loop(0, n_chunks, body, 0, unroll=True)
    o_ref[...] = _layer_norm(DN_ALPHA * x_ref[...] + 0.5 * acc_ref[...], g_ref[...], b_ref[...])


def _const_spec(shape):
    nd = len(shape)
    return pl.BlockSpec(shape, lambda i, _nd=nd: (0,) * _nd, pipeline_mode=pl.Buffered(1))


def _ffn_weights(wg, wu, wd, fc):
    d, ff = wg.shape
    n = ff // fc
    wg3 = wg.astype(BF16).reshape(d, n, fc).transpose(1, 0, 2)
    wu3 = wu.astype(BF16).reshape(d, n, fc).transpose(1, 0, 2)
    wd3 = wd.astype(BF16).reshape(n, fc, d)
    return wg3, wu3, wd3


def _ffn(x, w3, g, b, tm, mix=None):
    wg3, wu3, wd3 = w3
    n_rows, d = x.shape
    n_chunks, _, fc = wg3.shape
    assert n_rows % tm == 0
    row = lambda w: pl.BlockSpec((tm, w), lambda i: (i, 0))
    vec = lambda a: a.reshape(1, -1).astype(F32)
    args, specs = [x], [row(d)]
    scratch = [pltpu.VMEM((tm, d), F32)]
    if mix is not None:
        oa, orr, woa, wor, g2, b2 = mix
        args += [oa, orr, woa, wor, vec(g2), vec(b2)]
        specs += [row(oa.shape[1]), row(orr.shape[1]), _const_spec(woa.shape), _const_spec(wor.shape),
                  _const_spec((1, d)), _const_spec((1, d))]
        scratch.append(pltpu.VMEM((tm, d), F32))
    args += [wg3, wu3, wd3, vec(g), vec(b)]
    specs += [_const_spec(wg3.shape), _const_spec(wu3.shape), _const_spec(wd3.shape),
              _const_spec((1, d)), _const_spec((1, d))]
    return pl.pallas_call(
        functools.partial(_ffn_kernel, n_chunks=n_chunks, with_mix=mix is not None),
        grid=(n_rows // tm,),
        in_specs=specs,
        out_specs=row(d),
        out_shape=jax.ShapeDtypeStruct((n_rows, d), F32),
        scratch_shapes=scratch,
        compiler_params=pltpu.CompilerParams(dimension_semantics=("arbitrary",),
                                             vmem_limit_bytes=VMEM_LIMIT_BYTES),
        name="ffn_mix" if mix is not None else "ffn",
    )(*args)


A_Q = N_HEADS_A * HEAD_DIM
A_KV = N_KV_A * HEAD_DIM
I_Q = N_HEADS_IDX * HEAD_DIM
R_W = N_HEADS_R * HEAD_DIM
_SPLIT = (A_Q, A_KV, A_KV, I_Q, HEAD_DIM, N_HEADS_IDX, R_W, R_W, R_W, R_W)
_OFF = tuple(int(v) for v in np.cumsum((0,) + _SPLIT))
GROUP = N_HEADS_A // N_KV_A


def _q_perm():
    idx = []
    for j in range(GROUP):
        idx += list(range(j * HEAD_DIM, (j + 1) * HEAD_DIM))
        idx += list(range((GROUP + j) * HEAD_DIM, (GROUP + j + 1) * HEAD_DIM))
    return np.asarray(idx, np.int32)


def _proj_weight(w_in):
    seg = lambda i: w_in[:, _OFF[i]:_OFF[i + 1]]
    q = seg(0)[:, _q_perm()]
    ki = seg(4)
    wi = jnp.pad(seg(5), ((0, 0), (0, LANES - N_HEADS_IDX)))
    cols = [q, seg(1), seg(2), seg(3), ki, ki, wi, seg(6), seg(7), seg(8), seg(9)]
    return jnp.concatenate(cols, axis=1).astype(BF16)


def _proj_kernel(h_ref, w_ref, q_ref, k_ref, v_ref, kb_ref, vb_ref, qi_ref, ki_ref, kk_ref, wi_ref,
                 rq_ref, rf_ref, ri_ref, rg_ref):
    hb = h_ref[...].astype(BF16)
    col = [0]

    def take(width):
        out = _dot(hb, w_ref[:, col[0]:col[0] + width])
        col[0] += width
        return out

    scale = HEAD_DIM ** -0.5
    q_ref[...] = (take(A_Q) * scale).astype(BF16)
    k = take(A_KV)
    k_ref[...] = k
    kb_ref[...] = k.astype(BF16)
    v = take(A_KV)
    v_ref[...] = v
    vb_ref[...] = v.astype(BF16)
    qi_ref[...] = (take(I_Q) * scale).astype(BF16)
    kk = take(LANES)
    ki_ref[...] = kk[:, :HEAD_DIM]
    kk_ref[...] = kk.astype(BF16)
    wi_ref[...] = take(LANES) * (N_HEADS_IDX ** -0.5)
    rq_ref[...] = take(R_W)
    rf_ref[...] = take(R_W)
    ri_ref[...] = take(R_W)
    rg_ref[...] = take(R_W)


def _proj(h, w_cat, tm):
    n_rows, d = h.shape
    assert n_rows % tm == 0
    row = lambda w: pl.BlockSpec((tm, w), lambda i: (i, 0))
    outs = [(A_Q, BF16), (A_KV, F32), (A_KV, F32), (A_KV, BF16), (A_KV, BF16), (I_Q, BF16),
            (HEAD_DIM, F32), (LANES, BF16), (LANES, F32), (R_W, F32), (R_W, F32), (R_W, F32), (R_W, F32)]
    return pl.pallas_call(
        _proj_kernel,
        grid=(n_rows // tm,),
        in_specs=[row(d), _const_spec(w_cat.shape)],
        out_specs=[row(w) for w, _ in outs],
        out_shape=[jax.ShapeDtypeStruct((n_rows, w), dt) for w, dt in outs],
        compiler_params=pltpu.CompilerParams(dimension_semantics=("arbitrary",),
                                             vmem_limit_bytes=VMEM_LIMIT_BYTES),
        name="in_proj",
    )(h, w_cat)


def _ordered_float(code):
    key = code ^ jnp.int32(INT_MIN)
    bits = key ^ ((key >> 31) & jnp.int32(0x7FFFFFFF))
    f = pltpu.bitcast(bits, F32)
    return jnp.where(f != f, jnp.where(key >= 0, jnp.inf, NEG_INF), f)


def _count(mask):
    return jnp.sum(jnp.where(mask, 1.0, 0.0), axis=1, keepdims=True)


def _kth_largest(score_ref, n_sel):
    rows = score_ref.shape[0]

    def body(i, code):
        cand = code | jnp.left_shift(jnp.int32(1), 31 - i)
        cnt = _count(score_ref[...] >= _ordered_float(cand))
        return jnp.where(cnt >= n_sel, cand, code)

    return _ordered_float(lax.fori_loop(0, 32, body, jnp.zeros((rows, 1), I32)))


def _tie_cut(score_ref, thr, idx, n_sel, n_bits):
    rows, width = score_ref.shape
    need = n_sel - _count(score_ref[...] > thr)

    def body(i, j):
        cand = j | jnp.left_shift(jnp.int32(1), n_bits - 1 - i)
        cnt = _count((score_ref[...] == thr) & (idx < cand))
        return jnp.where(cnt <= need, cand, j)

    return lax.fori_loop(0, n_bits, body, jnp.zeros((rows, 1), I32))


def _select_mask(score_ref, idx, n_sel, n_bits, cut_ref):
    thr = _kth_largest(score_ref, n_sel)
    score = score_ref[...]
    eq = score == thr
    excess = ((_count(score > thr) + _count(eq)) > n_sel) & (thr != NEG_INF)
    cut_ref[...] = jnp.full(cut_ref.shape, 1 << n_bits, I32)

    @pl.when(jnp.max(jnp.where(excess, 1.0, 0.0)) > 0.0)
    def _():
        cut_ref[...] = _tie_cut(score_ref, thr, idx, n_sel, n_bits + 1)

    return (score > thr) | (eq & (idx < cut_ref[...]))


def _attn_prompt_kernel(q_ref, qi_ref, wi_ref, kk_ref, k_ref, v_ref, o_ref, score_ref, cut_ref, *, n_sel,
                        kv_step):
    tq = q_ref.shape[0]
    s_len = k_ref.shape[0]
    last = (pl.program_id(1) + 1) * tq - 1
    for n_kv in range(kv_step, s_len + 1, kv_step):
        @pl.when((last >= n_kv - kv_step) & (last < n_kv))
        def _(n_kv=n_kv):
            _attn_prompt_tile(q_ref, qi_ref, wi_ref, kk_ref.at[:n_kv], k_ref.at[:n_kv], v_ref.at[:n_kv], o_ref,
                              score_ref.at[:, :n_kv], cut_ref, n_sel)


def _attn_prompt_tile(q_ref, qi_ref, wi_ref, kk_ref, k_ref, v_ref, o_ref, score_ref, cut_ref, n_sel):
    tq = q_ref.shape[0]
    s_len = k_ref.shape[0]
    n_bits = (s_len - 1).bit_length()
    q_pos = pl.program_id(1) * tq + lax.broadcasted_iota(I32, (tq, 1), 0)
    key_pos = lax.broadcasted_iota(I32, (tq, s_len), 1)
    causal = key_pos <= q_pos
    low = lax.broadcasted_iota(I32, (1, LANES), 1) < HEAD_DIM
    zero = jnp.zeros((), BF16)

    kk = kk_ref[...]
    score = jnp.zeros((tq, s_len), F32)
    for j in range(I_Q // LANES):
        blk = qi_ref[:, j * LANES:(j + 1) * LANES]
        for half in range(2):
            h = 2 * j + half
            qm = jnp.where(low if half == 0 else ~low, blk, zero)
            score = score + wi_ref[:, h:h + 1] * jnp.maximum(_dot_nt(qm, kk), 0.0)
    score_ref[...] = jnp.where(causal, score, NEG_INF)
    sel = _select_mask(score_ref, key_pos, n_sel, n_bits, cut_ref) & causal
    bias = jnp.where(sel, 0.0, NEG_INF)

    kb = k_ref[...]
    vb = v_ref[...]
    for j in range(GROUP):
        blk = q_ref[:, j * LANES:(j + 1) * LANES]
        outs = []
        for half in range(2):
            qm = jnp.where(low if half == 0 else ~low, blk, zero)
            s = _dot_nt(qm, kb) + bias
            m = jnp.max(s, axis=1, keepdims=True)
            p = jnp.exp(s - m)
            l = jnp.sum(p, axis=1, keepdims=True)
            outs.append(_dot(p.astype(BF16), vb) / l)
        o_ref[:, j * LANES:(j + 1) * LANES] = jnp.where(low, outs[0], outs[1]).astype(o_ref.dtype)


def _attn_prompt(q, qi, wi, kk, kb, vb, batch, tq):
    n_rows = q.shape[0]
    s_len = n_rows // batch
    assert s_len & (s_len - 1) == 0 and s_len % tq == 0
    n_sel = min(TOPK_MAX, s_len // 4)
    nq = s_len // tq
    kv_step = min(KV_STEP, s_len)
    assert kv_step >= n_sel and kv_step % tq == 0 and s_len % kv_step == 0
    tile = lambda w: pl.BlockSpec((tq, w), lambda b, i: (b * nq + i, 0))
    full = lambda w: pl.BlockSpec((s_len, w), lambda b, i: (b, 0))
    return pl.pallas_call(
        functools.partial(_attn_prompt_kernel, n_sel=n_sel, kv_step=kv_step),
        grid=(batch, nq),
        in_specs=[tile(A_Q), tile(I_Q), tile(LANES), full(LANES), full(A_KV), full(A_KV)],
        out_specs=tile(A_Q),
        out_shape=jax.ShapeDtypeStruct((n_rows, A_Q), BF16),
        scratch_shapes=[pltpu.VMEM((tq, s_len), F32), pltpu.VMEM((tq, 1), I32)],
        compiler_params=pltpu.CompilerParams(dimension_semantics=("arbitrary", "arbitrary"),
                                             vmem_limit_bytes=VMEM_LIMIT_BYTES),
        name="attn_prompt",
    )(q, qi, wi, kk, kb, vb)


def _hgrn_consts(c):
    t = np.arange(c)
    tri = (t[None, :] <= t[:, None]).astype(np.float32)
    masks = []
    hs = c // 2
    while hs >= 1:
        blk = t // (2 * hs)
        right = (t % (2 * hs)) >= hs
        masks.append(((blk[:, None] == blk[None, :]) & right[:, None] & ~right[None, :]).astype(np.float32))
        hs //= 2
    return jnp.asarray(tri, BF16), jnp.asarray(np.stack(masks, 0), F32)


def _head_block_ones():
    d = np.arange(LANES) // HEAD_DIM
    return (d[:, None] == d[None, :]).astype(np.float32)


def _level_reference(b_ref, hs):
    c = b_ref.shape[0]
    row = lambda r, n: jnp.broadcast_to(b_ref[r:r + 1, :], (n, LANES))
    if 2 * hs >= SUBLANES:
        return jnp.concatenate([row(s + hs - 1, 2 * hs) for s in range(0, c, 2 * hs)], axis=0)
    sub = lax.broadcasted_iota(I32, (SUBLANES, 1), 0)
    tiles = []
    for s in range(0, c, SUBLANES):
        e = row(s + hs - 1, SUBLANES)
        for off in range(2 * hs, SUBLANES, 2 * hs):
            e = jnp.where(sub >= off, row(s + off + hs - 1, SUBLANES), e)
        tiles.append(e)
    return jnp.concatenate(tiles, axis=0)


def _hgrn_prompt_kernel(rq_ref, rf_ref, ri_ref, rg_ref, lb_ref, ng_ref, tri_ref, msk_ref, bo_ref,
                        o_ref, s_out_ref, st_ref, b_ref):
    c = rq_ref.shape[0]
    n_levels = msk_ref.shape[0]
    step = pl.program_id(1)

    @pl.when(step == 0)
    def _():
        st_ref[...] = jnp.zeros_like(st_ref)

    bo = bo_ref[...]
    bo_bf = bo.astype(BF16)
    same_head = bo > 0.5
    low = lax.broadcasted_iota(I32, (1, LANES), 1) < HEAD_DIM
    for p in range(rq_ref.shape[1] // LANES):
        sl = slice(p * LANES, (p + 1) * LANES)
        lb = lb_ref[:, sl]
        f = lb + (1.0 - lb) * jax.nn.sigmoid(rf_ref[:, sl])
        q = rq_ref[:, sl]
        k = 1.0 - f
        v = ri_ref[:, sl]
        vb = v.astype(BF16)

        b = _dot_exact_lhs(tri_ref[...], jnp.log(f))
        b_ref[p] = b
        st = st_ref[p]
        o = _dot_nt((q * jnp.exp(b)).astype(BF16), st.astype(BF16))
        o = o + _dot((q * k).astype(BF16), bo_bf) * v

        a0 = jnp.zeros((c, c), F32)
        a1 = jnp.zeros((c, c), F32)
        for lv in range(n_levels):
            diff = b - _level_reference(b_ref.at[p], c >> (lv + 1))
            qdl = q * jnp.exp(jnp.minimum(diff, 0.0))
            kdl = (k * jnp.exp(jnp.minimum(-diff, 0.0))).astype(BF16)
            msk = msk_ref[lv] > 0.5
            a0 = a0 + jnp.where(msk, _dot_nt(jnp.where(low, qdl, 0.0).astype(BF16), kdl), 0.0)
            a1 = a1 + jnp.where(msk, _dot_nt(jnp.where(low, 0.0, qdl).astype(BF16), kdl), 0.0)
        o = o + jnp.where(low, _dot(a0.astype(BF16), vb), _dot(a1.astype(BF16), vb))

        b_last = b[c - 1:c, :]
        kdec = (k * jnp.exp(jnp.minimum(b_last - b, 0.0))).astype(BF16)
        st_new = st * jnp.exp(b_last) + jnp.where(same_head, _dot_tn(vb, kdec), 0.0)
        st_ref[p] = st_new

        ms = _dot_exact_rhs(o * o, bo_bf) * (1.0 / HEAD_DIM)
        g = rg_ref[:, sl]
        o_ref[:, sl] = (o * lax.rsqrt(ms + RMS_EPS) * ng_ref[:, sl] * (g * jax.nn.sigmoid(g))).astype(o_ref.dtype)

        @pl.when(step == pl.num_programs(1) - 1)
        def _(p=p, st_new=st_new):
            s_out_ref[p] = st_new.T


def _hgrn_prompt(rq, rf, ri, rg, lb, norm_g, batch, chunk):
    n_rows, width = rq.shape
    t_len = n_rows // batch
    assert t_len % chunk == 0 and chunk & (chunk - 1) == 0 and chunk >= SUBLANES
    nc = t_len // chunk
    n_pairs = width // LANES
    tri, msk = _hgrn_consts(chunk)
    bo = jnp.asarray(_head_block_ones(), F32)
    tile = pl.BlockSpec((chunk, width), lambda b, i: (b * nc + i, 0))
    const = lambda a: pl.BlockSpec(a.shape, lambda b, i, _nd=a.ndim: (0,) * _nd)
    lb2, ng2 = lb.reshape(1, -1), norm_g.reshape(1, -1)
    o, s_pairs = pl.pallas_call(
        _hgrn_prompt_kernel,
        grid=(batch, nc),
        in_specs=[tile, tile, tile, tile, const(lb2), const(ng2), const(tri), const(msk), const(bo)],
        out_specs=[tile, pl.BlockSpec((None, n_pairs, LANES, LANES), lambda b, i: (b, 0, 0, 0))],
        out_shape=[jax.ShapeDtypeStruct((n_rows, width), BF16),
                   jax.ShapeDtypeStruct((batch, n_pairs, LANES, LANES), F32)],
        scratch_shapes=[pltpu.VMEM((n_pairs, LANES, LANES), F32), pltpu.VMEM((n_pairs, chunk, LANES), F32)],
        compiler_params=pltpu.CompilerParams(dimension_semantics=("arbitrary", "arbitrary"),
                                             vmem_limit_bytes=VMEM_LIMIT_BYTES),
        name="hgrn_prompt",
    )(rq, rf, ri, rg, lb2, ng2, tri, msk, bo)
    s0 = s_pairs[:, :, :HEAD_DIM, :HEAD_DIM]
    s1 = s_pairs[:, :, HEAD_DIM:, HEAD_DIM:]
    return o, jnp.stack([s0, s1], axis=2).reshape(batch, N_HEADS_R, HEAD_DIM, HEAD_DIM)


def _column(row, eye):
    return jnp.sum(jnp.where(eye, row, 0.0), axis=1, keepdims=True)


def _eye(n):
    return lax.broadcasted_iota(I32, (n, n), 0) == lax.broadcasted_iota(I32, (n, n), 1)


PAGE_UNROLL = 8


def _fetch_pages(pt_ref, b, n_pages, src_ref, dst_ref, sem):
    def start(p, carry):
        pltpu.make_async_copy(src_ref.at[pt_ref[b, p]], dst_ref.at[p], sem).start()
        return carry

    lax.fori_loop(0, n_pages, start, 0, unroll=PAGE_UNROLL)


def _wait_pages(n_pages, src_ref, dst_ref, sem):
    def wait(p, carry):
        pltpu.make_async_copy(src_ref.at[0], dst_ref.at[p], sem).wait()
        return carry

    lax.fori_loop(0, n_pages, wait, 0, unroll=PAGE_UNROLL)


def _prefetch_sequence(pt_ref, n_pages, streams):
    b = pl.program_id(0)
    slot = lax.rem(b, 2)

    @pl.when(b == 0)
    def _():
        for src, buf, sem in streams:
            _fetch_pages(pt_ref, 0, n_pages, src, buf.at[0], sem.at[0])

    @pl.when(b + 1 < pl.num_programs(0))
    def _():
        for src, buf, sem in streams:
            _fetch_pages(pt_ref, b + 1, n_pages, src, buf.at[1 - slot], sem.at[1 - slot])

    return slot


def _score_sample_kernel(pt_ref, qi_ref, wi_ref, kin_ref, cache_ref, o_ref, kbuf2, sem):
    b = pl.program_id(0)
    n_pages = kbuf2.shape[1]
    slot = _prefetch_sequence(pt_ref, n_pages, [(cache_ref, kbuf2, sem)])
    kbuf = kbuf2.at[slot]
    qi = qi_ref[...]
    wcol = _column(wi_ref[pl.ds(b, 1), :][:, :N_HEADS_IDX], _eye(N_HEADS_IDX))
    lane0 = lax.broadcasted_iota(I32, (1, PAGE_SIZE), 1) == 0
    ki_new = kin_ref[pl.ds(b, 1), :].astype(BF16).astype(F32)
    lg_new = jnp.sum(qi.astype(F32) * ki_new, axis=1, keepdims=True)
    sc_new = jnp.sum(wcol * jnp.maximum(lg_new, 0.0), axis=0, keepdims=True)
    o_ref[...] = jnp.full(o_ref.shape, NEG_INF, F32)
    o_ref[pl.ds(n_pages, 1), :] = jnp.where(lane0, sc_new, NEG_INF)
    _wait_pages(n_pages, cache_ref, kbuf, sem.at[slot])

    def page(p, carry):
        lg = _dot(qi, kbuf[p].astype(BF16))
        o_ref[pl.ds(p, 1), :] = jnp.sum(wcol * jnp.maximum(lg, 0.0), axis=0, keepdims=True)
        return carry

    lax.fori_loop(0, n_pages, page, 0, unroll=PAGE_UNROLL)


def _score_sample(page_table, qi3, wi, ki_new, cache_kiT, rows_pad):
    batch, n_pages = page_table.shape
    return pl.pallas_call(
        _score_sample_kernel,
        grid_spec=pltpu.PrefetchScalarGridSpec(
            num_scalar_prefetch=1,
            grid=(batch,),
            in_specs=[pl.BlockSpec((None, N_HEADS_IDX, HEAD_DIM), lambda b, pt: (b, 0, 0)),
                      pl.BlockSpec(wi.shape, lambda b, pt: (0, 0)),
                      pl.BlockSpec(ki_new.shape, lambda b, pt: (0, 0)),
                      pl.BlockSpec(memory_space=pl.ANY)],
            out_specs=pl.BlockSpec((None, rows_pad, PAGE_SIZE), lambda b, pt: (b, 0, 0)),
            scratch_shapes=[pltpu.VMEM((2, n_pages, HEAD_DIM, PAGE_SIZE), F32), pltpu.SemaphoreType.DMA((2,))]),
        out_shape=jax.ShapeDtypeStruct((batch, rows_pad, PAGE_SIZE), F32),
        compiler_params=pltpu.CompilerParams(dimension_semantics=("arbitrary",),
                                             vmem_limit_bytes=VMEM_LIMIT_BYTES),
        name="score_sample",
    )(page_table, qi3, wi, ki_new, cache_kiT)


def _select_sample_kernel(s_ref, o_ref, cut_ref, *, n_sel):
    rows, width = s_ref.shape
    n_bits = (width - 1).bit_length()
    idx = lax.broadcasted_iota(I32, (rows, width), 1)
    sel = _select_mask(s_ref, idx, n_sel, n_bits, cut_ref)
    o_ref[...] = jnp.where(sel, 0.0, NEG_INF)


def _select_sample(scores, n_sel):
    rows, width = scores.shape
    return pl.pallas_call(
        functools.partial(_select_sample_kernel, n_sel=n_sel),
        out_shape=jax.ShapeDtypeStruct((rows, width), F32),
        scratch_shapes=[pltpu.VMEM((rows, 1), I32)],
        compiler_params=pltpu.CompilerParams(vmem_limit_bytes=VMEM_LIMIT_BYTES),
        name="select_sample",
    )(scores)


def _attn_sample_kernel(pt_ref, q_ref, bias_ref, kn_ref, vn_ref, ck_ref, cv_ref, o_ref,
                        kbuf2, vbuf2, s_ref, ksem, vsem):
    b = pl.program_id(0)
    n_pages = kbuf2.shape[1]
    slot = _prefetch_sequence(pt_ref, n_pages, [(ck_ref, kbuf2, ksem), (cv_ref, vbuf2, vsem)])
    kbuf, vbuf = kbuf2.at[slot], vbuf2.at[slot]
    q = q_ref[...]
    lane_low = lax.broadcasted_iota(I32, (1, LANES), 1) < HEAD_DIM
    lane0 = lax.broadcasted_iota(I32, (1, PAGE_SIZE), 1) == 0
    kn = kn_ref[pl.ds(b, 1), :]
    vn = vn_ref[pl.ds(b, 1), :]
    s_new = jnp.sum(q.astype(F32) * kn.astype(BF16).astype(F32), axis=1, keepdims=True)
    s_ref[n_pages] = jnp.where(lane0, s_new, NEG_INF) + bias_ref[pl.ds(n_pages, 1), :]
    _wait_pages(n_pages, ck_ref, kbuf, ksem.at[slot])

    def qk(p, m):
        kt = kbuf[p].reshape(A_KV, PAGE_SIZE).astype(BF16)
        s = _dot(q, kt) + bias_ref[pl.ds(p, 1), :]
        s_ref[p] = s
        return jnp.maximum(m, s)

    m = lax.fori_loop(0, n_pages, qk, s_ref[n_pages], unroll=PAGE_UNROLL)
    m = jnp.max(m, axis=1, keepdims=True)
    _wait_pages(n_pages, cv_ref, vbuf, vsem.at[slot])

    def pv(p, carry):
        acc, l = carry
        e = jnp.exp(s_ref[p] - m)
        vt = vbuf[p].reshape(A_KV, PAGE_SIZE).astype(BF16)
        return acc + _dot_nt(e.astype(BF16), vt), l + e

    e_new = jnp.exp(s_ref[n_pages] - m)
    acc0 = jnp.sum(e_new, axis=1, keepdims=True).astype(BF16).astype(F32) * vn.astype(BF16).astype(F32)
    acc, l = lax.fori_loop(0, n_pages, pv, (acc0, e_new), unroll=PAGE_UNROLL)
    out = acc / jnp.sum(l, axis=1, keepdims=True)
    o_ref[...] = jnp.where(lane_low, out[:GROUP], out[GROUP:]).astype(o_ref.dtype)


def _attn_sample(page_table, q8, bias3, k_new, v_new, cache_kT, cache_vT):
    batch, n_pages = page_table.shape
    rows_pad = bias3.shape[1]
    return pl.pallas_call(
        _attn_sample_kernel,
        grid_spec=pltpu.PrefetchScalarGridSpec(
            num_scalar_prefetch=1,
            grid=(batch,),
            in_specs=[pl.BlockSpec((None, N_HEADS_A, LANES), lambda b, pt: (b, 0, 0)),
                      pl.BlockSpec((None, rows_pad, PAGE_SIZE), lambda b, pt: (b, 0, 0)),
                      pl.BlockSpec(k_new.shape, lambda b, pt: (0, 0)),
                      pl.BlockSpec(v_new.shape, lambda b, pt: (0, 0)),
                      pl.BlockSpec(memory_space=pl.ANY),
                      pl.BlockSpec(memory_space=pl.ANY)],
            out_specs=pl.BlockSpec((None, GROUP, LANES), lambda b, pt: (b, 0, 0)),
            scratch_shapes=[pltpu.VMEM((2, n_pages, N_KV_A, HEAD_DIM, PAGE_SIZE), F32),
                            pltpu.VMEM((2, n_pages, N_KV_A, HEAD_DIM, PAGE_SIZE), F32),
                            pltpu.VMEM((rows_pad, N_HEADS_A, PAGE_SIZE), F32),
                            pltpu.SemaphoreType.DMA((2,)), pltpu.SemaphoreType.DMA((2,))]),
        out_shape=jax.ShapeDtypeStruct((batch, GROUP, LANES), F32),
        compiler_params=pltpu.CompilerParams(dimension_semantics=("arbitrary",),
                                             vmem_limit_bytes=VMEM_LIMIT_BYTES),
        name="attn_sample",
    )(page_table, q8, bias3, k_new, v_new, cache_kT, cache_vT)


def _hgrn_sample_kernel(rq_ref, rf_ref, ri_ref, rg_ref, lb_ref, ng_ref, s_ref, o_ref, s_out_ref):
    b = pl.program_id(0)
    eye = _eye(HEAD_DIM)
    row = lambda ref: ref[pl.ds(b, 1), :]
    lb = lb_ref[...]
    f = lb + (1.0 - lb) * jax.nn.sigmoid(row(rf_ref))
    q, k, v, g = row(rq_ref), 1.0 - f, row(ri_ref), row(rg_ref)
    gate = ng_ref[...] * (g * jax.nn.sigmoid(g))
    outs = []
    for h in range(N_HEADS_R):
        sl = slice(h * HEAD_DIM, (h + 1) * HEAD_DIM)
        s_new = _column(f[:, sl], eye) * s_ref[h] + _column(k[:, sl], eye) * v[:, sl]
        s_out_ref[h] = s_new
        o = jnp.sum(_column(q[:, sl], eye) * s_new, axis=0, keepdims=True)
        outs.append(o * lax.rsqrt(jnp.mean(o * o, axis=1, keepdims=True) + RMS_EPS))
    o_ref[pl.ds(b, 1), :] = jnp.concatenate(outs, axis=1) * gate


def _hgrn_sample(rq, rf, ri, rg, lb, norm_g, state):
    batch = rq.shape[0]
    full = lambda a: pl.BlockSpec(a.shape, lambda b, _nd=a.ndim: (0,) * _nd)
    st = pl.BlockSpec((None, N_HEADS_R, HEAD_DIM, HEAD_DIM), lambda b: (b, 0, 0, 0))
    lb2, ng2 = lb.reshape(1, -1), norm_g.reshape(1, -1)
    return pl.pallas_call(
        _hgrn_sample_kernel,
        grid=(batch,),
        in_specs=[full(rq), full(rf), full(ri), full(rg), full(lb2), full(ng2), st],
        out_specs=[full(rq), st],
        out_shape=[jax.ShapeDtypeStruct(rq.shape, F32), jax.ShapeDtypeStruct(state.shape, F32)],
        compiler_params=pltpu.CompilerParams(dimension_semantics=("arbitrary",)),
        name="hgrn_sample",
    )(rq, rf, ri, rg, lb2, ng2, state)


FFN_CHUNK = 256
ROW_TILE = 512
Q_TILE = 256
KV_STEP = 512
HGRN_TILE = 128


def kernel(x_prompt, x_sample, cache_k, cache_v, cache_kidx, state_hgrn, page_table, w_in, w_out, hgrn_lb,
           hgrn_norm_g, ffn1_wg, ffn1_wu, ffn1_wd, ffn2_wg, ffn2_wu, ffn2_wd, ln1_g, ln1_b, ln2_g, ln2_b,
           ln3_g, ln3_b):
    assert w_in.shape[0] == DEPTH == 1
    batch, seq, d_model = x_prompt.shape
    dec_batch, dec_seq, _ = x_sample.shape
    assert dec_seq == 1
    n_pages = page_table.shape[1]
    l = 0

    lb = jnp.cumsum(jax.nn.softmax(hgrn_lb.astype(F32), axis=0), axis=0)[l]
    w1 = _ffn_weights(ffn1_wg[l], ffn1_wu[l], ffn1_wd[l], FFN_CHUNK)
    w2 = _ffn_weights(ffn2_wg[l], ffn2_wu[l], ffn2_wd[l], FFN_CHUNK)
    w_cat = _proj_weight(w_in[l])
    wo = w_out[l].astype(BF16)
    wo_a, wo_r = wo[:A_Q][_q_perm()], wo[A_Q:]

    xp = x_prompt.reshape(batch * seq, d_model)
    hp = _ffn(xp, w1, ln1_g[l], ln1_b[l], ROW_TILE)
    (q, k, v, kb, vb, qi, ki, kk, wi, rq, rf, ri, rg) = _proj(hp, w_cat, ROW_TILE)
    oa = _attn_prompt(q, qi, wi, kk, kb, vb, batch, Q_TILE)
    orr, sp = _hgrn_prompt(rq, rf, ri, rg, lb, hgrn_norm_g[l], batch, HGRN_TILE)
    yp = _ffn(hp, w2, ln3_g[l], ln3_b[l], ROW_TILE, mix=(oa, orr, wo_a, wo_r, ln2_g[l], ln2_b[l]))

    xs = x_sample.reshape(dec_batch, d_model)
    hs = _ffn(xs, w1, ln1_g[l], ln1_b[l], dec_batch)
    (qs, ks, vs, _, _, qis, kis, _, wis, rqs, rfs, ris, rgs) = _proj(hs, w_cat, dec_batch)
    rows_pad = -(-(n_pages + 1) // SUBLANES) * SUBLANES
    cache_kiT = jnp.transpose(cache_kidx[l], (0, 2, 1))
    cache_kT = jnp.transpose(cache_k[l], (0, 2, 3, 1))
    cache_vT = jnp.transpose(cache_v[l], (0, 2, 3, 1))
    scores = _score_sample(page_table, qis.reshape(dec_batch, N_HEADS_IDX, HEAD_DIM), wis, kis, cache_kiT,
                           rows_pad)
    n_sel = min(TOPK_MAX, (n_pages * PAGE_SIZE + 1) // 4)
    bias = _select_sample(scores.reshape(dec_batch, rows_pad * PAGE_SIZE), n_sel)
    q_heads = qs.reshape(dec_batch, GROUP, 2, HEAD_DIM).transpose(0, 2, 1, 3).reshape(
        dec_batch, N_HEADS_A, HEAD_DIM)
    zeros = jnp.zeros_like(q_heads)
    q8 = jnp.concatenate([jnp.concatenate([q_heads[:, :GROUP], zeros[:, :GROUP]], -1),
                          jnp.concatenate([zeros[:, GROUP:], q_heads[:, GROUP:]], -1)], 1)
    oas = _attn_sample(page_table, q8, bias.reshape(dec_batch, rows_pad, PAGE_SIZE), ks, vs,
                       cache_kT, cache_vT).reshape(dec_batch, A_Q).astype(BF16)
    ors, ss = _hgrn_sample(rqs, rfs, ris, rgs, lb, hgrn_norm_g[l], state_hgrn[l])
    ys = _ffn(hs, w2, ln3_g[l], ln3_b[l], dec_batch,
              mix=(oas, ors.astype(BF16), wo_a, wo_r, ln2_g[l], ln2_b[l]))

    return (yp.reshape(batch, seq, d_model), ys.reshape(dec_batch, 1, d_model),
            k.reshape(1, batch, seq, N_KV_A, HEAD_DIM), v.reshape(1, batch, seq, N_KV_A, HEAD_DIM),
            ki.reshape(1, batch, seq, HEAD_DIM), sp[None],
            ks.reshape(1, dec_batch, 1, N_KV_A, HEAD_DIM), vs.reshape(1, dec_batch, 1, N_KV_A, HEAD_DIM),
            kis.reshape(1, dec_batch, 1, HEAD_DIM), ss[None])
```

```python
import functools

import numpy as np
import jax
import jax.numpy as jnp
from jax import lax
from jax.experimental import pallas as pl
from jax.experimental.pallas import tpu as pltpu

F32 = jnp.float32
BF16 = jnp.bfloat16
I32 = jnp.int32

DEPTH = 1
N_HEADS_A = 8
N_KV_A = 2
HEAD_DIM = 64
N_HEADS_IDX = 8
N_HEADS_R = 8
TOPK_MAX = 256
PAGE_SIZE = 128
LN_EPS = 1e-5
RMS_EPS = 1e-6
DN_ALPHA = (2 * DEPTH) ** 0.25

LANES = 128
SUBLANES = 8
VMEM_LIMIT_BYTES = 56 * 1024 * 1024

NEG_INF = float("-inf")
INT_MIN = -(2 ** 31)


def _dot(a, b):
    return jnp.dot(a, b, preferred_element_type=F32)


def _dot_nt(a, b):
    return lax.dot_general(a, b, (((1,), (1,)), ((), ())), preferred_element_type=F32)


def _dot_tn(a, b):
    return lax.dot_general(a, b, (((0,), (0,)), ((), ())), preferred_element_type=F32)


def _split3(x):
    hi = x.astype(BF16)
    r1 = x - hi.astype(F32)
    mid = r1.astype(BF16)
    lo = (r1 - mid.astype(F32)).astype(BF16)
    return hi, mid, lo


def _dot_exact_lhs(m_bf, x):
    hi, mid, lo = _split3(x)
    return _dot(m_bf, hi) + _dot(m_bf, mid) + _dot(m_bf, lo)


def _dot_exact_rhs(x, m_bf):
    hi, mid, lo = _split3(x)
    return _dot(hi, m_bf) + _dot(mid, m_bf) + _dot(lo, m_bf)


def _layer_norm(x, g, b):
    mu = jnp.mean(x, -1, keepdims=True)
    xc = x - mu
    var = jnp.mean(xc * xc, -1, keepdims=True)
    return xc * lax.rsqrt(var + LN_EPS) * g + b


def _ffn_kernel(*refs, fc, with_mix):
    if with_mix:
        (h_ref, oa_ref, or_ref, woa_ref, wor_ref, g2_ref, b2_ref,
         wg_ref, wu_ref, wd_ref, g_ref, b_ref, o_ref, acc_ref, x_ref) = refs
        m = _dot(oa_ref[...], woa_ref[...]) + _dot(or_ref[...], wor_ref[...])
        x_ref[...] = _layer_norm(DN_ALPHA * h_ref[...] + m, g2_ref[...], b2_ref[...])
    else:
        x_ref, wg_ref, wu_ref, wd_ref, g_ref, b_ref, o_ref, acc_ref = refs
    xb = x_ref[...].astype(BF16)
    acc_ref[...] = jnp.zeros_like(acc_ref)
    for c0 in range(0, wg_ref.shape[1], fc):
        gg = _dot(xb, wg_ref[:, c0:c0 + fc])
        uu = _dot(xb, wu_ref[:, c0:c0 + fc])
        hh = (gg * jax.nn.sigmoid(gg) * uu).astype(BF16)
        acc_ref[...] += _dot(hh, wd_ref[c0:c0 + fc, :])
    o_ref[...] = _layer_norm(DN_ALPHA * x_ref[...] + 0.5 * acc_ref[...], g_ref[...], b_ref[...])


def _const_spec(shape):
    nd = len(shape)
    return pl.BlockSpec(shape, lambda i, _nd=nd: (0,) * _nd, pipeline_mode=pl.Buffered(1))


def _ffn_weights(wg, wu, wd):
    return wg.astype(BF16), wu.astype(BF16), wd.astype(BF16)


def _ffn(x, w3, g, b, tm, mix=None):
    wg3, wu3, wd3 = w3
    n_rows, d = x.shape
    assert n_rows % tm == 0 and wg3.shape[1] % FFN_CHUNK == 0
    row = lambda w: pl.BlockSpec((tm, w), lambda i: (i, 0))
    vec = lambda a: a.reshape(1, -1).astype(F32)
    args, specs = [x], [row(d)]
    scratch = [pltpu.VMEM((tm, d), F32)]
    if mix is not None:
        oa, orr, woa, wor, g2, b2 = mix
        args += [oa, orr, woa, wor, vec(g2), vec(b2)]
        specs += [row(oa.shape[1]), row(orr.shape[1]), _const_spec(woa.shape), _const_spec(wor.shape),
                  _const_spec((1, d)), _const_spec((1, d))]
        scratch.append(pltpu.VMEM((tm, d), F32))
    args += [wg3, wu3, wd3, vec(g), vec(b)]
    specs += [_const_spec(wg3.shape), _const_spec(wu3.shape), _const_spec(wd3.shape),
              _const_spec((1, d)), _const_spec((1, d))]
    return pl.pallas_call(
        functools.partial(_ffn_kernel, fc=FFN_CHUNK, with_mix=mix is not None),
        grid=(n_rows // tm,),
        in_specs=specs,
        out_specs=row(d),
        out_shape=jax.ShapeDtypeStruct((n_rows, d), F32),
        scratch_shapes=scratch,
        compiler_params=pltpu.CompilerParams(dimension_semantics=("arbitrary",),
                                             vmem_limit_bytes=VMEM_LIMIT_BYTES),
        name="ffn_mix" if mix is not None else "ffn",
    )(*args)


A_Q = N_HEADS_A * HEAD_DIM
A_KV = N_KV_A * HEAD_DIM
I_Q = N_HEADS_IDX * HEAD_DIM
R_W = N_HEADS_R * HEAD_DIM
_SPLIT = (A_Q, A_KV, A_KV, I_Q, HEAD_DIM, N_HEADS_IDX, R_W, R_W, R_W, R_W)
_OFF = tuple(int(v) for v in np.cumsum((0,) + _SPLIT))
GROUP = N_HEADS_A // N_KV_A


def _q_perm():
    idx = []
    for j in range(GROUP):
        idx += list(range(j * HEAD_DIM, (j + 1) * HEAD_DIM))
        idx += list(range((GROUP + j) * HEAD_DIM, (GROUP + j + 1) * HEAD_DIM))
    return np.asarray(idx, np.int32)


def _proj_weight(w_in):
    seg = lambda i: w_in[:, _OFF[i]:_OFF[i + 1]]
    q = seg(0)[:, _q_perm()]
    ki = seg(4)
    wi = jnp.pad(seg(5), ((0, 0), (0, LANES - N_HEADS_IDX)))
    cols = [q, seg(1), seg(2), seg(3), ki, ki, wi, seg(6), seg(7), seg(8), seg(9)]
    return jnp.concatenate(cols, axis=1).astype(BF16)


def _proj_kernel(h_ref, w_ref, q_ref, k_ref, v_ref, kb_ref, vb_ref, qi_ref, ki_ref, kk_ref, wi_ref,
                 rq_ref, rf_ref, ri_ref, rg_ref, *, keys_on_lanes):
    tr = (lambda a: a.T) if keys_on_lanes else (lambda a: a)
    hb = h_ref[...].astype(BF16)
    col = [0]

    def take(width):
        out = _dot(hb, w_ref[:, col[0]:col[0] + width])
        col[0] += width
        return out

    scale = HEAD_DIM ** -0.5
    q_ref[...] = (take(A_Q) * scale).astype(BF16)
    k = tr(take(A_KV))
    k_ref[...] = k
    kb_ref[...] = k.astype(BF16)
    v = take(A_KV)
    v_ref[...] = tr(v)
    vb_ref[...] = v.astype(BF16)
    qi_ref[...] = (take(I_Q) * scale).astype(BF16)
    kk = tr(take(LANES))
    ki_ref[...] = kk[:HEAD_DIM, :] if keys_on_lanes else kk[:, :HEAD_DIM]
    kk_ref[...] = kk.astype(BF16)
    wi_ref[...] = take(LANES) * (N_HEADS_IDX ** -0.5)
    rq_ref[...] = take(R_W)
    rf_ref[...] = take(R_W)
    ri_ref[...] = take(R_W)
    rg_ref[...] = take(R_W)


def _proj(h, w_cat, tm, seq=None):
    n_rows, d = h.shape
    assert n_rows % tm == 0
    row = lambda w: pl.BlockSpec((tm, w), lambda i: (i, 0))
    outs = [(A_Q, BF16), (A_KV, F32), (A_KV, F32), (A_KV, BF16), (A_KV, BF16), (I_Q, BF16),
            (HEAD_DIM, F32), (LANES, BF16), (LANES, F32), (R_W, F32), (R_W, F32), (R_W, F32), (R_W, F32)]
    out_specs = [row(w) for w, _ in outs]
    out_shape = [jax.ShapeDtypeStruct((n_rows, w), dt) for w, dt in outs]
    if seq is not None:
        assert seq % tm == 0 and n_rows % seq == 0
        per_seq = seq // tm
        for i in (1, 2, 3, 6, 7):
            w, dt = outs[i]
            out_specs[i] = pl.BlockSpec((None, w, tm), lambda i, _p=per_seq: (i // _p, 0, i % _p))
            out_shape[i] = jax.ShapeDtypeStruct((n_rows // seq, w, seq), dt)
    return pl.pallas_call(
        functools.partial(_proj_kernel, keys_on_lanes=seq is not None),
        grid=(n_rows // tm,),
        in_specs=[row(d), _const_spec(w_cat.shape)],
        out_specs=out_specs,
        out_shape=out_shape,
        compiler_params=pltpu.CompilerParams(dimension_semantics=("arbitrary",),
                                             vmem_limit_bytes=VMEM_LIMIT_BYTES),
        name="in_proj",
    )(h, w_cat)


def _ordered_float(code):
    key = code ^ jnp.int32(INT_MIN)
    bits = key ^ ((key >> 31) & jnp.int32(0x7FFFFFFF))
    f = pltpu.bitcast(bits, F32)
    return jnp.where(f != f, jnp.where(key >= 0, jnp.inf, NEG_INF), f)


def _count(mask):
    return jnp.sum(jnp.where(mask, 1.0, 0.0), axis=1, keepdims=True)


def _kth_largest(score_ref, n_sel):
    rows = score_ref.shape[0]

    def body(i, code):
        cand = code | jnp.left_shift(jnp.int32(1), 31 - i)
        cnt = _count(score_ref[...] >= _ordered_float(cand))
        return jnp.where(cnt >= n_sel, cand, code)

    return _ordered_float(lax.fori_loop(0, 32, body, jnp.zeros((rows, 1), I32)))


def _tie_cut(score_ref, thr, idx, n_sel, n_bits):
    rows, width = score_ref.shape
    need = n_sel - _count(score_ref[...] > thr)

    def body(i, j):
        cand = j | jnp.left_shift(jnp.int32(1), n_bits - 1 - i)
        cnt = _count((score_ref[...] == thr) & (idx < cand))
        return jnp.where(cnt <= need, cand, j)

    return lax.fori_loop(0, n_bits, body, jnp.zeros((rows, 1), I32))


def _select_mask(score_ref, idx, n_sel, n_bits, cut_ref):
    thr = _kth_largest(score_ref, n_sel)
    score = score_ref[...]
    eq = score == thr
    excess = ((_count(score > thr) + _count(eq)) > n_sel) & (thr != NEG_INF)
    cut_ref[...] = jnp.full(cut_ref.shape, 1 << n_bits, I32)

    @pl.when(jnp.max(jnp.where(excess, 1.0, 0.0)) > 0.0)
    def _():
        cut_ref[...] = _tie_cut(score_ref, thr, idx, n_sel, n_bits + 1)

    return (score > thr) | (eq & (idx < cut_ref[...]))


def _attn_prompt_kernel(q_ref, qi_ref, wi_ref, kk_ref, k_ref, v_ref, o_ref, score_ref, cut_ref, *, n_sel,
                        kv_step):
    tq = q_ref.shape[0]
    s_len = v_ref.shape[0]
    last = (pl.program_id(1) + 1) * tq - 1
    for n_kv in range(kv_step, s_len + 1, kv_step):
        @pl.when((last >= n_kv - kv_step) & (last < n_kv))
        def _(n_kv=n_kv):
            _attn_prompt_tile(q_ref, qi_ref, wi_ref, kk_ref.at[:, :n_kv], k_ref.at[:, :n_kv], v_ref.at[:n_kv],
                              o_ref, score_ref.at[:, :n_kv], cut_ref, n_sel)


def _attn_prompt_tile(q_ref, qi_ref, wi_ref, kk_ref, k_ref, v_ref, o_ref, score_ref, cut_ref, n_sel):
    tq = q_ref.shape[0]
    s_len = v_ref.shape[0]
    n_bits = (s_len - 1).bit_length()
    q_pos = pl.program_id(1) * tq + lax.broadcasted_iota(I32, (tq, 1), 0)
    key_pos = lax.broadcasted_iota(I32, (tq, s_len), 1)
    causal = key_pos <= q_pos
    low = lax.broadcasted_iota(I32, (1, LANES), 1) < HEAD_DIM
    zero = jnp.zeros((), BF16)

    kk = kk_ref[...]
    score = jnp.zeros((tq, s_len), F32)
    for j in range(I_Q // LANES):
        blk = qi_ref[:, j * LANES:(j + 1) * LANES]
        for half in range(2):
            h = 2 * j + half
            qm = jnp.where(low if half == 0 else ~low, blk, zero)
            score = score + wi_ref[:, h:h + 1] * jnp.maximum(_dot(qm, kk), 0.0)
    score_ref[...] = jnp.where(causal, score, NEG_INF)
    sel = _select_mask(score_ref, key_pos, n_sel, n_bits, cut_ref) & causal
    bias = jnp.where(sel, 0.0, NEG_INF)

    kb = k_ref[...]
    vb = v_ref[...]
    for j in range(GROUP):
        blk = q_ref[:, j * LANES:(j + 1) * LANES]
        outs = []
        for half in range(2):
            qm = jnp.where(low if half == 0 else ~low, blk, zero)
            s = _dot(qm, kb) + bias
            m = jnp.max(s, axis=1, keepdims=True)
            p = jnp.exp(s - m)
            l = jnp.sum(p, axis=1, keepdims=True)
            outs.append(_dot(p.astype(BF16), vb) / l)
        o_ref[:, j * LANES:(j + 1) * LANES] = jnp.where(low, outs[0], outs[1]).astype(o_ref.dtype)


def _attn_prompt(q, qi, wi, kk_t, kb_t, vb, batch, tq):
    n_rows = q.shape[0]
    s_len = n_rows // batch
    assert s_len & (s_len - 1) == 0 and s_len % tq == 0
    n_sel = min(TOPK_MAX, s_len // 4)
    nq = s_len // tq
    kv_step = min(KV_STEP, s_len)
    assert kv_step >= n_sel and kv_step % tq == 0 and s_len % kv_step == 0
    tile = lambda w: pl.BlockSpec((tq, w), lambda b, i: (b * nq + i, 0))
    full = lambda w: pl.BlockSpec((s_len, w), lambda b, i: (b, 0))
    full_t = lambda w: pl.BlockSpec((None, w, s_len), lambda b, i: (b, 0, 0))
    return pl.pallas_call(
        functools.partial(_attn_prompt_kernel, n_sel=n_sel, kv_step=kv_step),
        grid=(batch, nq),
        in_specs=[tile(A_Q), tile(I_Q), tile(LANES), full_t(LANES), full_t(A_KV), full(A_KV)],
        out_specs=tile(A_Q),
        out_shape=jax.ShapeDtypeStruct((n_rows, A_Q), BF16),
        scratch_shapes=[pltpu.VMEM((tq, s_len), F32), pltpu.VMEM((tq, 1), I32)],
        compiler_params=pltpu.CompilerParams(dimension_semantics=("arbitrary", "arbitrary"),
                                             vmem_limit_bytes=VMEM_LIMIT_BYTES),
        name="attn_prompt",
    )(q, qi, wi, kk_t, kb_t, vb)


def _hgrn_consts(c):
    t = np.arange(c)
    tri = (t[None, :] <= t[:, None]).astype(np.float32)
    masks = []
    hs = c // 2
    while hs >= 1:
        blk = t // (2 * hs)
        right = (t % (2 * hs)) >= hs
        masks.append(((blk[:, None] == blk[None, :]) & right[:, None] & ~right[None, :]).astype(np.float32))
        hs //= 2
    return jnp.asarray(tri, BF16), jnp.asarray(np.stack(masks, 0), F32)


def _head_block_ones():
    d = np.arange(LANES) // HEAD_DIM
    return (d[:, None] == d[None, :]).astype(np.float32)


def _level_reference(b_ref, hs):
    c = b_ref.shape[0]
    row = lambda r, n: jnp.broadcast_to(b_ref[r:r + 1, :], (n, LANES))
    if 2 * hs >= SUBLANES:
        return jnp.concatenate([row(s + hs - 1, 2 * hs) for s in range(0, c, 2 * hs)], axis=0)
    sub = lax.broadcasted_iota(I32, (SUBLANES, 1), 0)
    tiles = []
    for s in range(0, c, SUBLANES):
        e = row(s + hs - 1, SUBLANES)
        for off in range(2 * hs, SUBLANES, 2 * hs):
            e = jnp.where(sub >= off, row(s + off + hs - 1, SUBLANES), e)
        tiles.append(e)
    return jnp.concatenate(tiles, axis=0)


def _hgrn_prompt_kernel(rq_ref, rf_ref, ri_ref, rg_ref, lb_ref, ng_ref, tri_ref, msk_ref, bo_ref,
                        o_ref, s_out_ref, st_ref, b_ref):
    c = rq_ref.shape[0]
    n_levels = msk_ref.shape[0]
    step = pl.program_id(1)

    @pl.when(step == 0)
    def _():
        st_ref[...] = jnp.zeros_like(st_ref)

    bo = bo_ref[...]
    bo_bf = bo.astype(BF16)
    same_head = bo > 0.5
    low = lax.broadcasted_iota(I32, (1, LANES), 1) < HEAD_DIM
    for p in range(rq_ref.shape[1] // LANES):
        sl = slice(p * LANES, (p + 1) * LANES)
        lb = lb_ref[:, sl]
        f = lb + (1.0 - lb) * jax.nn.sigmoid(rf_ref[:, sl])
        q = rq_ref[:, sl]
        k = 1.0 - f
        v = ri_ref[:, sl]
        vb = v.astype(BF16)

        b = _dot_exact_lhs(tri_ref[...], jnp.log(f))
        b_ref[p] = b
        st = st_ref[p]
        o = _dot_nt((q * jnp.exp(b)).astype(BF16), st.astype(BF16))
        o = o + _dot((q * k).astype(BF16), bo_bf) * v

        a0 = jnp.zeros((c, c), F32)
        a1 = jnp.zeros((c, c), F32)
        for lv in range(n_levels):
            diff = b - _level_reference(b_ref.at[p], c >> (lv + 1))
            qdl = q * jnp.exp(jnp.minimum(diff, 0.0))
            kdl = (k * jnp.exp(jnp.minimum(-diff, 0.0))).astype(BF16)
            msk = msk_ref[lv] > 0.5
            a0 = a0 + jnp.where(msk, _dot_nt(jnp.where(low, qdl, 0.0).astype(BF16), kdl), 0.0)
            a1 = a1 + jnp.where(msk, _dot_nt(jnp.where(low, 0.0, qdl).astype(BF16), kdl), 0.0)
        o = o + jnp.where(low, _dot(a0.astype(BF16), vb), _dot(a1.astype(BF16), vb))

        b_last = b[c - 1:c, :]
        kdec = (k * jnp.exp(jnp.minimum(b_last - b, 0.0))).astype(BF16)
        st_new = st * jnp.exp(b_last) + jnp.where(same_head, _dot_tn(vb, kdec), 0.0)
        st_ref[p] = st_new

        ms = _dot_exact_rhs(o * o, bo_bf) * (1.0 / HEAD_DIM)
        g = rg_ref[:, sl]
        o_ref[:, sl] = (o * lax.rsqrt(ms + RMS_EPS) * ng_ref[:, sl] * (g * jax.nn.sigmoid(g))).astype(o_ref.dtype)

        @pl.when(step == pl.num_programs(1) - 1)
        def _(p=p, st_new=st_new):
            s_out_ref[p] = st_new.T


def _hgrn_prompt(rq, rf, ri, rg, lb, norm_g, batch, chunk):
    n_rows, width = rq.shape
    t_len = n_rows // batch
    assert t_len % chunk == 0 and chunk & (chunk - 1) == 0 and chunk >= SUBLANES
    nc = t_len // chunk
    n_pairs = width // LANES
    tri, msk = _hgrn_consts(chunk)
    bo = jnp.asarray(_head_block_ones(), F32)
    tile = pl.BlockSpec((chunk, width), lambda b, i: (b * nc + i, 0))
    const = lambda a: pl.BlockSpec(a.shape, lambda b, i, _nd=a.ndim: (0,) * _nd)
    lb2, ng2 = lb.reshape(1, -1), norm_g.reshape(1, -1)
    o, s_pairs = pl.pallas_call(
        _hgrn_prompt_kernel,
        grid=(batch, nc),
        in_specs=[tile, tile, tile, tile, const(lb2), const(ng2), const(tri), const(msk), const(bo)],
        out_specs=[tile, pl.BlockSpec((None, n_pairs, LANES, LANES), lambda b, i: (b, 0, 0, 0))],
        out_shape=[jax.ShapeDtypeStruct((n_rows, width), BF16),
                   jax.ShapeDtypeStruct((batch, n_pairs, LANES, LANES), F32)],
        scratch_shapes=[pltpu.VMEM((n_pairs, LANES, LANES), F32), pltpu.VMEM((n_pairs, chunk, LANES), F32)],
        compiler_params=pltpu.CompilerParams(dimension_semantics=("arbitrary", "arbitrary"),
                                             vmem_limit_bytes=VMEM_LIMIT_BYTES),
        name="hgrn_prompt",
    )(rq, rf, ri, rg, lb2, ng2, tri, msk, bo)
    s0 = s_pairs[:, :, :HEAD_DIM, :HEAD_DIM]
    s1 = s_pairs[:, :, HEAD_DIM:, HEAD_DIM:]
    return o, jnp.stack([s0, s1], axis=2).reshape(batch, N_HEADS_R, HEAD_DIM, HEAD_DIM)


def _column(row, eye):
    return jnp.sum(jnp.where(eye, row, 0.0), axis=1, keepdims=True)


def _eye(n):
    return lax.broadcasted_iota(I32, (n, n), 0) == lax.broadcasted_iota(I32, (n, n), 1)


PAGE_UNROLL = 16


def _fetch_pages(pt_ref, b, n_pages, src_ref, dst_ref, sem):
    def start(p, carry):
        pltpu.make_async_copy(src_ref.at[pt_ref[b, p]], dst_ref.at[p], sem).start()
        return carry

    lax.fori_loop(0, n_pages, start, 0, unroll=PAGE_UNROLL)


def _wait_pages(n_pages, src_ref, dst_ref, sem):
    def wait(p, carry):
        pltpu.make_async_copy(src_ref.at[0], dst_ref.at[p], sem).wait()
        return carry

    lax.fori_loop(0, n_pages, wait, 0, unroll=PAGE_UNROLL)


def _prefetch_sequence(pt_ref, n_pages, streams):
    b = pl.program_id(0)
    slot = lax.rem(b, 2)

    @pl.when(b == 0)
    def _():
        for src, buf, sem in streams:
            _fetch_pages(pt_ref, 0, n_pages, src, buf.at[0], sem.at[0])

    @pl.when(b + 1 < pl.num_programs(0))
    def _():
        for src, buf, sem in streams:
            _fetch_pages(pt_ref, b + 1, n_pages, src, buf.at[1 - slot], sem.at[1 - slot])

    return slot


def _score_sample_kernel(pt_ref, qi_ref, wi_ref, kin_ref, cache_ref, o_ref, kbuf2, sem):
    b = pl.program_id(0)
    n_pages = kbuf2.shape[1]
    slot = _prefetch_sequence(pt_ref, n_pages, [(cache_ref, kbuf2, sem)])
    kbuf = kbuf2.at[slot]
    qi = qi_ref[...]
    wcol = _column(wi_ref[pl.ds(b, 1), :][:, :N_HEADS_IDX], _eye(N_HEADS_IDX))
    lane0 = lax.broadcasted_iota(I32, (1, PAGE_SIZE), 1) == 0
    ki_new = kin_ref[pl.ds(b, 1), :].astype(BF16).astype(F32)
    lg_new = jnp.sum(qi.astype(F32) * ki_new, axis=1, keepdims=True)
    sc_new = jnp.sum(wcol * jnp.maximum(lg_new, 0.0), axis=0, keepdims=True)
    o_ref[...] = jnp.full(o_ref.shape, NEG_INF, F32)
    o_ref[pl.ds(n_pages, 1), :] = jnp.where(lane0, sc_new, NEG_INF)
    _wait_pages(n_pages, cache_ref, kbuf, sem.at[slot])

    def page(p, carry):
        lg = _dot(qi, kbuf[p].astype(BF16))
        o_ref[pl.ds(p, 1), :] = jnp.sum(wcol * jnp.maximum(lg, 0.0), axis=0, keepdims=True)
        return carry

    lax.fori_loop(0, n_pages, page, 0, unroll=PAGE_UNROLL)


def _score_sample(page_table, qi3, wi, ki_new, cache_kiT, rows_pad):
    batch, n_pages = page_table.shape
    return pl.pallas_call(
        _score_sample_kernel,
        grid_spec=pltpu.PrefetchScalarGridSpec(
            num_scalar_prefetch=1,
            grid=(batch,),
            in_specs=[pl.BlockSpec((None, N_HEADS_IDX, HEAD_DIM), lambda b, pt: (b, 0, 0)),
                      pl.BlockSpec(wi.shape, lambda b, pt: (0, 0)),
                      pl.BlockSpec(ki_new.shape, lambda b, pt: (0, 0)),
                      pl.BlockSpec(memory_space=pl.ANY)],
            out_specs=pl.BlockSpec((None, rows_pad, PAGE_SIZE), lambda b, pt: (b, 0, 0)),
            scratch_shapes=[pltpu.VMEM((2, n_pages, HEAD_DIM, PAGE_SIZE), F32), pltpu.SemaphoreType.DMA((2,))]),
        out_shape=jax.ShapeDtypeStruct((batch, rows_pad, PAGE_SIZE), F32),
        compiler_params=pltpu.CompilerParams(dimension_semantics=("arbitrary",),
                                             vmem_limit_bytes=VMEM_LIMIT_BYTES),
        name="score_sample",
    )(page_table, qi3, wi, ki_new, cache_kiT)


def _select_sample_kernel(s_ref, o_ref, cut_ref, *, n_sel):
    rows, width = s_ref.shape
    n_bits = (width - 1).bit_length()
    idx = lax.broadcasted_iota(I32, (rows, width), 1)
    sel = _select_mask(s_ref, idx, n_sel, n_bits, cut_ref)
    o_ref[...] = jnp.where(sel, 0.0, NEG_INF)


def _select_sample(scores, n_sel):
    rows, width = scores.shape
    return pl.pallas_call(
        functools.partial(_select_sample_kernel, n_sel=n_sel),
        out_shape=jax.ShapeDtypeStruct((rows, width), F32),
        scratch_shapes=[pltpu.VMEM((rows, 1), I32)],
        compiler_params=pltpu.CompilerParams(vmem_limit_bytes=VMEM_LIMIT_BYTES),
        name="select_sample",
    )(scores)


def _attn_sample_kernel(pt_ref, q_ref, bias_ref, kn_ref, vn_ref, ck_ref, cv_ref, o_ref,
                        kbuf2, vbuf2, s_ref, ksem, vsem):
    b = pl.program_id(0)
    n_pages = kbuf2.shape[1]
    slot = _prefetch_sequence(pt_ref, n_pages, [(ck_ref, kbuf2, ksem), (cv_ref, vbuf2, vsem)])
    kbuf, vbuf = kbuf2.at[slot], vbuf2.at[slot]
    q = q_ref[...]
    lane_low = lax.broadcasted_iota(I32, (1, LANES), 1) < HEAD_DIM
    lane0 = lax.broadcasted_iota(I32, (1, PAGE_SIZE), 1) == 0
    kn = kn_ref[pl.ds(b, 1), :]
    vn = vn_ref[pl.ds(b, 1), :]
    s_new = jnp.sum(q.astype(F32) * kn.astype(BF16).astype(F32), axis=1, keepdims=True)
    s_ref[n_pages] = jnp.where(lane0, s_new, NEG_INF) + bias_ref[pl.ds(n_pages, 1), :]
    _wait_pages(n_pages, ck_ref, kbuf, ksem.at[slot])

    def qk(p, m):
        kt = kbuf[p].reshape(A_KV, PAGE_SIZE).astype(BF16)
        s = _dot(q, kt) + bias_ref[pl.ds(p, 1), :]
        s_ref[p] = s
        return jnp.maximum(m, s)

    m = lax.fori_loop(0, n_pages, qk, s_ref[n_pages], unroll=PAGE_UNROLL)
    m = jnp.max(m, axis=1, keepdims=True)
    _wait_pages(n_pages, cv_ref, vbuf, vsem.at[slot])

    def pv(p, carry):
        acc, l = carry
        e = jnp.exp(s_ref[p] - m)
        vt = vbuf[p].reshape(A_KV, PAGE_SIZE).astype(BF16)
        return acc + _dot_nt(e.astype(BF16), vt), l + e

    e_new = jnp.exp(s_ref[n_pages] - m)
    acc0 = jnp.sum(e_new, axis=1, keepdims=True).astype(BF16).astype(F32) * vn.astype(BF16).astype(F32)
    acc, l = lax.fori_loop(0, n_pages, pv, (acc0, e_new), unroll=PAGE_UNROLL)
    out = acc / jnp.sum(l, axis=1, keepdims=True)
    o_ref[...] = jnp.where(lane_low, out[:GROUP], out[GROUP:]).astype(o_ref.dtype)


def _attn_sample(page_table, q8, bias3, k_new, v_new, cache_kT, cache_vT):
    batch, n_pages = page_table.shape
    rows_pad = bias3.shape[1]
    return pl.pallas_call(
        _attn_sample_kernel,
        grid_spec=pltpu.PrefetchScalarGridSpec(
            num_scalar_prefetch=1,
            grid=(batch,),
            in_specs=[pl.BlockSpec((None, N_HEADS_A, LANES), lambda b, pt: (b, 0, 0)),
                      pl.BlockSpec((None, rows_pad, PAGE_SIZE), lambda b, pt: (b, 0, 0)),
                      pl.BlockSpec(k_new.shape, lambda b, pt: (0, 0)),
                      pl.BlockSpec(v_new.shape, lambda b, pt: (0, 0)),
                      pl.BlockSpec(memory_space=pl.ANY),
                      pl.BlockSpec(memory_space=pl.ANY)],
            out_specs=pl.BlockSpec((None, GROUP, LANES), lambda b, pt: (b, 0, 0)),
            scratch_shapes=[pltpu.VMEM((2, n_pages, N_KV_A, HEAD_DIM, PAGE_SIZE), F32),
                            pltpu.VMEM((2, n_pages, N_KV_A, HEAD_DIM, PAGE_SIZE), F32),
                            pltpu.VMEM((rows_pad, N_HEADS_A, PAGE_SIZE), F32),
                            pltpu.SemaphoreType.DMA((2,)), pltpu.SemaphoreType.DMA((2,))]),
        out_shape=jax.ShapeDtypeStruct((batch, GROUP, LANES), F32),
        compiler_params=pltpu.CompilerParams(dimension_semantics=("arbitrary",),
                                             vmem_limit_bytes=VMEM_LIMIT_BYTES),
        name="attn_sample",
    )(page_table, q8, bias3, k_new, v_new, cache_kT, cache_vT)


def _hgrn_sample_kernel(rq_ref, rf_ref, ri_ref, rg_ref, lb_ref, ng_ref, s_ref, o_ref, s_out_ref):
    b = pl.program_id(0)
    eye = _eye(HEAD_DIM)
    row = lambda ref: ref[pl.ds(b, 1), :]
    lb = lb_ref[...]
    f = lb + (1.0 - lb) * jax.nn.sigmoid(row(rf_ref))
    q, k, v, g = row(rq_ref), 1.0 - f, row(ri_ref), row(rg_ref)
    gate = ng_ref[...] * (g * jax.nn.sigmoid(g))
    outs = []
    for h in range(N_HEADS_R):
        sl = slice(h * HEAD_DIM, (h + 1) * HEAD_DIM)
        s_new = _column(f[:, sl], eye) * s_ref[h] + _column(k[:, sl], eye) * v[:, sl]
        s_out_ref[h] = s_new
        o = jnp.sum(_column(q[:, sl], eye) * s_new, axis=0, keepdims=True)
        outs.append(o * lax.rsqrt(jnp.mean(o * o, axis=1, keepdims=True) + RMS_EPS))
    o_ref[pl.ds(b, 1), :] = jnp.concatenate(outs, axis=1) * gate


def _hgrn_sample(rq, rf, ri, rg, lb, norm_g, state):
    batch = rq.shape[0]
    full = lambda a: pl.BlockSpec(a.shape, lambda b, _nd=a.ndim: (0,) * _nd)
    st = pl.BlockSpec((None, N_HEADS_R, HEAD_DIM, HEAD_DIM), lambda b: (b, 0, 0, 0))
    lb2, ng2 = lb.reshape(1, -1), norm_g.reshape(1, -1)
    return pl.pallas_call(
        _hgrn_sample_kernel,
        grid=(batch,),
        in_specs=[full(rq), full(rf), full(ri), full(rg), full(lb2), full(ng2), st],
        out_specs=[full(rq), st],
        out_shape=[jax.ShapeDtypeStruct(rq.shape, F32), jax.ShapeDtypeStruct(state.shape, F32)],
        compiler_params=pltpu.CompilerParams(dimension_semantics=("arbitrary",)),
        name="hgrn_sample",
    )(rq, rf, ri, rg, lb2, ng2, state)


FFN_CHUNK = 256
ROW_TILE = 512
Q_TILE = 256
KV_STEP = 256
HGRN_TILE = 128


def kernel(x_prompt, x_sample, cache_k, cache_v, cache_kidx, state_hgrn, page_table, w_in, w_out, hgrn_lb,
           hgrn_norm_g, ffn1_wg, ffn1_wu, ffn1_wd, ffn2_wg, ffn2_wu, ffn2_wd, ln1_g, ln1_b, ln2_g, ln2_b,
           ln3_g, ln3_b):
    assert w_in.shape[0] == DEPTH == 1
    batch, seq, d_model = x_prompt.shape
    dec_batch, dec_seq, _ = x_sample.shape
    assert dec_seq == 1
    n_pages = page_table.shape[1]
    l = 0

    lb = jnp.cumsum(jax.nn.softmax(hgrn_lb.astype(F32), axis=0), axis=0)[l]
    w1 = _ffn_weights(ffn1_wg[l], ffn1_wu[l], ffn1_wd[l])
    w2 = _ffn_weights(ffn2_wg[l], ffn2_wu[l], ffn2_wd[l])
    w_cat = _proj_weight(w_in[l])
    wo = w_out[l].astype(BF16)
    wo_a, wo_r = wo[:A_Q][_q_perm()], wo[A_Q:]

    xp = x_prompt.reshape(batch * seq, d_model)
    hp = _ffn(xp, w1, ln1_g[l], ln1_b[l], ROW_TILE)
    (q, k_t, v_t, kb_t, vb, qi, ki_t, kk_t, wi, rq, rf, ri, rg) = _proj(hp, w_cat, ROW_TILE, seq=seq)
    k = k_t.reshape(batch, N_KV_A, HEAD_DIM, seq).transpose(0, 3, 1, 2)
    v = v_t.reshape(batch, N_KV_A, HEAD_DIM, seq).transpose(0, 3, 1, 2)
    ki = ki_t.transpose(0, 2, 1)
    oa = _attn_prompt(q, qi, wi, kk_t, kb_t, vb, batch, Q_TILE)
    orr, sp = _hgrn_prompt(rq, rf, ri, rg, lb, hgrn_norm_g[l], batch, HGRN_TILE)
    yp = _ffn(hp, w2, ln3_g[l], ln3_b[l], ROW_TILE, mix=(oa, orr, wo_a, wo_r, ln2_g[l], ln2_b[l]))

    xs = x_sample.reshape(dec_batch, d_model)
    hs = _ffn(xs, w1, ln1_g[l], ln1_b[l], dec_batch)
    (qs, ks, vs, _, _, qis, kis, _, wis, rqs, rfs, ris, rgs) = _proj(hs, w_cat, dec_batch)
    rows_pad = -(-(n_pages + 1) // SUBLANES) * SUBLANES
    cache_kiT = jnp.transpose(cache_kidx[l], (0, 2, 1))
    cache_kT = jnp.transpose(cache_k[l], (0, 2, 3, 1))
    cache_vT = jnp.transpose(cache_v[l], (0, 2, 3, 1))
    scores = _score_sample(page_table, qis.reshape(dec_batch, N_HEADS_IDX, HEAD_DIM), wis, kis, cache_kiT,
                           rows_pad)
    n_sel = min(TOPK_MAX, (n_pages * PAGE_SIZE + 1) // 4)
    bias = _select_sample(scores.reshape(dec_batch, rows_pad * PAGE_SIZE), n_sel)
    q_heads = qs.reshape(dec_batch, GROUP, 2, HEAD_DIM).transpose(0, 2, 1, 3).reshape(
        dec_batch, N_HEADS_A, HEAD_DIM)
    zeros = jnp.zeros_like(q_heads)
    q8 = jnp.concatenate([jnp.concatenate([q_heads[:, :GROUP], zeros[:, :GROUP]], -1),
                          jnp.concatenate([zeros[:, GROUP:], q_heads[:, GROUP:]], -1)], 1)
    oas = _attn_sample(page_table, q8, bias.reshape(dec_batch, rows_pad, PAGE_SIZE), ks, vs,
                       cache_kT, cache_vT).reshape(dec_batch, A_Q).astype(BF16)
    ors, ss = _hgrn_sample(rqs, rfs, ris, rgs, lb, hgrn_norm_g[l], state_hgrn[l])
    ys = _ffn(hs, w2, ln3_g[l], ln3_b[l], dec_batch,
              mix=(oas, ors.astype(BF16), wo_a, wo_r, ln2_g[l], ln2_b[l]))

    return (yp.reshape(batch, seq, d_model), ys.reshape(dec_batch, 1, d_model),
            k[None], v[None], ki[None], sp[None],
            ks.reshape(1, dec_batch, 1, N_KV_A, HEAD_DIM), vs.reshape(1, dec_batch, 1, N_KV_A, HEAD_DIM),
            kis.reshape(1, dec_batch, 1, HEAD_DIM), ss[None])
```

```python
import functools

import numpy as np
import jax
import jax.numpy as jnp
from jax import lax
from jax.experimental import pallas as pl
from jax.experimental.pallas import tpu as pltpu

F32 = jnp.float32
BF16 = jnp.bfloat16
I32 = jnp.int32

DEPTH = 1
N_HEADS_A = 8
N_KV_A = 2
HEAD_DIM = 64
N_HEADS_IDX = 8
N_HEADS_R = 8
TOPK_MAX = 256
PAGE_SIZE = 128
LN_EPS = 1e-5
RMS_EPS = 1e-6
DN_ALPHA = (2 * DEPTH) ** 0.25

LANES = 128
SUBLANES = 8
VMEM_LIMIT_BYTES = 56 * 1024 * 1024

NEG_INF = float("-inf")
INT_MIN = -(2 ** 31)


def _dot(a, b):
    return jnp.dot(a, b, preferred_element_type=F32)


def _dot_nt(a, b):
    return lax.dot_general(a, b, (((1,), (1,)), ((), ())), preferred_element_type=F32)


def _dot_tn(a, b):
    return lax.dot_general(a, b, (((0,), (0,)), ((), ())), preferred_element_type=F32)


def _split3(x):
    hi = x.astype(BF16)
    r1 = x - hi.astype(F32)
    mid = r1.astype(BF16)
    lo = (r1 - mid.astype(F32)).astype(BF16)
    return hi, mid, lo


def _dot_exact_lhs(m_bf, x):
    hi, mid, lo = _split3(x)
    return _dot(m_bf, hi) + _dot(m_bf, mid) + _dot(m_bf, lo)


def _dot_exact_rhs(x, m_bf):
    hi, mid, lo = _split3(x)
    return _dot(hi, m_bf) + _dot(mid, m_bf) + _dot(lo, m_bf)


def _layer_norm(x, g, b):
    mu = jnp.mean(x, -1, keepdims=True)
    xc = x - mu
    var = jnp.mean(xc * xc, -1, keepdims=True)
    return xc * lax.rsqrt(var + LN_EPS) * g + b


def _ffn_kernel(*refs, fc, with_mix):
    if with_mix:
        (h_ref, oa_ref, or_ref, woa_ref, wor_ref, g2_ref, b2_ref,
         wg_ref, wu_ref, wd_ref, g_ref, b_ref, o_ref, acc_ref, x_ref) = refs
        m = _dot(oa_ref[...], woa_ref[...]) + _dot(or_ref[...], wor_ref[...])
        x_ref[...] = _layer_norm(DN_ALPHA * h_ref[...] + m, g2_ref[...], b2_ref[...])
    else:
        x_ref, wg_ref, wu_ref, wd_ref, g_ref, b_ref, o_ref, acc_ref = refs
    xb = x_ref[...].astype(BF16)
    acc_ref[...] = jnp.zeros_like(acc_ref)
    for c0 in range(0, wg_ref.shape[1], fc):
        gg = _dot(xb, wg_ref[:, c0:c0 + fc])
        uu = _dot(xb, wu_ref[:, c0:c0 + fc])
        hh = (gg * jax.nn.sigmoid(gg) * uu).astype(BF16)
        acc_ref[...] += _dot(hh, wd_ref[c0:c0 + fc, :])
    o_ref[...] = _layer_norm(DN_ALPHA * x_ref[...] + 0.5 * acc_ref[...], g_ref[...], b_ref[...])


def _const_spec(shape):
    nd = len(shape)
    return pl.BlockSpec(shape, lambda i, _nd=nd: (0,) * _nd, pipeline_mode=pl.Buffered(1))


def _ffn_weights(wg, wu, wd):
    return wg.astype(BF16), wu.astype(BF16), wd.astype(BF16)


def _ffn(x, w3, g, b, tm, mix=None):
    wg3, wu3, wd3 = w3
    n_rows, d = x.shape
    assert n_rows % tm == 0 and wg3.shape[1] % FFN_CHUNK == 0
    row = lambda w: pl.BlockSpec((tm, w), lambda i: (i, 0))
    vec = lambda a: a.reshape(1, -1).astype(F32)
    args, specs = [x], [row(d)]
    scratch = [pltpu.VMEM((tm, d), F32)]
    if mix is not None:
        oa, orr, woa, wor, g2, b2 = mix
        args += [oa, orr, woa, wor, vec(g2), vec(b2)]
        specs += [row(oa.shape[1]), row(orr.shape[1]), _const_spec(woa.shape), _const_spec(wor.shape),
                  _const_spec((1, d)), _const_spec((1, d))]
        scratch.append(pltpu.VMEM((tm, d), F32))
    args += [wg3, wu3, wd3, vec(g), vec(b)]
    specs += [_const_spec(wg3.shape), _const_spec(wu3.shape), _const_spec(wd3.shape),
              _const_spec((1, d)), _const_spec((1, d))]
    return pl.pallas_call(
        functools.partial(_ffn_kernel, fc=FFN_CHUNK, with_mix=mix is not None),
        grid=(n_rows // tm,),
        in_specs=specs,
        out_specs=row(d),
        out_shape=jax.ShapeDtypeStruct((n_rows, d), F32),
        scratch_shapes=scratch,
        compiler_params=pltpu.CompilerParams(dimension_semantics=("arbitrary",),
                                             vmem_limit_bytes=VMEM_LIMIT_BYTES),
        name="ffn_mix" if mix is not None else "ffn",
    )(*args)


A_Q = N_HEADS_A * HEAD_DIM
A_KV = N_KV_A * HEAD_DIM
I_Q = N_HEADS_IDX * HEAD_DIM
R_W = N_HEADS_R * HEAD_DIM
_SPLIT = (A_Q, A_KV, A_KV, I_Q, HEAD_DIM, N_HEADS_IDX, R_W, R_W, R_W, R_W)
_OFF = tuple(int(v) for v in np.cumsum((0,) + _SPLIT))
GROUP = N_HEADS_A // N_KV_A


def _q_perm():
    idx = []
    for j in range(GROUP):
        idx += list(range(j * HEAD_DIM, (j + 1) * HEAD_DIM))
        idx += list(range((GROUP + j) * HEAD_DIM, (GROUP + j + 1) * HEAD_DIM))
    return np.asarray(idx, np.int32)


def _proj_weight(w_in):
    seg = lambda i: w_in[:, _OFF[i]:_OFF[i + 1]]
    q = seg(0)[:, _q_perm()]
    ki = seg(4)
    wi = jnp.pad(seg(5), ((0, 0), (0, LANES - N_HEADS_IDX)))
    cols = [q, seg(1), seg(2), seg(3), ki, ki, wi, seg(6), seg(7), seg(8), seg(9)]
    return jnp.concatenate(cols, axis=1).astype(BF16)


def _proj_kernel(h_ref, w_ref, q_ref, k_ref, v_ref, kb_ref, vb_ref, qi_ref, ki_ref, kk_ref, wi_ref,
                 rq_ref, rf_ref, ri_ref, rg_ref, *, keys_on_lanes):
    tr = (lambda a: a.T) if keys_on_lanes else (lambda a: a)
    hb = h_ref[...].astype(BF16)
    col = [0]

    def take(width):
        out = _dot(hb, w_ref[:, col[0]:col[0] + width])
        col[0] += width
        return out

    scale = HEAD_DIM ** -0.5
    q_ref[...] = (take(A_Q) * scale).astype(BF16)
    k = tr(take(A_KV))
    k_ref[...] = k
    kb_ref[...] = k.astype(BF16)
    v = take(A_KV)
    v_ref[...] = tr(v)
    vb_ref[...] = v.astype(BF16)
    qi_ref[...] = (take(I_Q) * scale).astype(BF16)
    kk = tr(take(LANES))
    ki_ref[...] = kk[:HEAD_DIM, :] if keys_on_lanes else kk[:, :HEAD_DIM]
    kk_ref[...] = kk.astype(BF16)
    wi_ref[...] = take(LANES) * (N_HEADS_IDX ** -0.5)
    rq_ref[...] = take(R_W)
    rf_ref[...] = take(R_W)
    ri_ref[...] = take(R_W)
    rg_ref[...] = take(R_W)


def _proj(h, w_cat, tm, seq=None):
    n_rows, d = h.shape
    assert n_rows % tm == 0
    row = lambda w: pl.BlockSpec((tm, w), lambda i: (i, 0))
    outs = [(A_Q, BF16), (A_KV, F32), (A_KV, F32), (A_KV, BF16), (A_KV, BF16), (I_Q, BF16),
            (HEAD_DIM, F32), (LANES, BF16), (LANES, F32), (R_W, F32), (R_W, F32), (R_W, F32), (R_W, F32)]
    out_specs = [row(w) for w, _ in outs]
    out_shape = [jax.ShapeDtypeStruct((n_rows, w), dt) for w, dt in outs]
    if seq is not None:
        assert seq % tm == 0 and n_rows % seq == 0
        per_seq = seq // tm
        for i in (1, 2, 3, 6, 7):
            w, dt = outs[i]
            out_specs[i] = pl.BlockSpec((None, w, tm), lambda i, _p=per_seq: (i // _p, 0, i % _p))
            out_shape[i] = jax.ShapeDtypeStruct((n_rows // seq, w, seq), dt)
    return pl.pallas_call(
        functools.partial(_proj_kernel, keys_on_lanes=seq is not None),
        grid=(n_rows // tm,),
        in_specs=[row(d), _const_spec(w_cat.shape)],
        out_specs=out_specs,
        out_shape=out_shape,
        compiler_params=pltpu.CompilerParams(dimension_semantics=("arbitrary",),
                                             vmem_limit_bytes=VMEM_LIMIT_BYTES),
        name="in_proj",
    )(h, w_cat)


def _ordered_float(code):
    key = code ^ jnp.int32(INT_MIN)
    bits = key ^ ((key >> 31) & jnp.int32(0x7FFFFFFF))
    f = pltpu.bitcast(bits, F32)
    return jnp.where(f != f, jnp.where(key >= 0, jnp.inf, NEG_INF), f)


def _count(mask):
    return jnp.sum(jnp.where(mask, 1.0, 0.0), axis=1, keepdims=True)


def _kth_largest(score_ref, n_sel):
    rows = score_ref.shape[0]

    def body(i, code):
        cand = code | jnp.left_shift(jnp.int32(1), 31 - i)
        cnt = _count(score_ref[...] >= _ordered_float(cand))
        return jnp.where(cnt >= n_sel, cand, code)

    return _ordered_float(lax.fori_loop(0, 32, body, jnp.zeros((rows, 1), I32)))


def _tie_cut(score_ref, thr, idx, n_sel, n_bits):
    rows, width = score_ref.shape
    need = n_sel - _count(score_ref[...] > thr)

    def body(i, j):
        cand = j | jnp.left_shift(jnp.int32(1), n_bits - 1 - i)
        cnt = _count((score_ref[...] == thr) & (idx < cand))
        return jnp.where(cnt <= need, cand, j)

    return lax.fori_loop(0, n_bits, body, jnp.zeros((rows, 1), I32))


def _select_mask(score_ref, idx, n_sel, n_bits, cut_ref):
    thr = _kth_largest(score_ref, n_sel)
    score = score_ref[...]
    eq = score == thr
    excess = ((_count(score > thr) + _count(eq)) > n_sel) & (thr != NEG_INF)
    cut_ref[...] = jnp.full(cut_ref.shape, 1 << n_bits, I32)

    @pl.when(jnp.max(jnp.where(excess, 1.0, 0.0)) > 0.0)
    def _():
        cut_ref[...] = _tie_cut(score_ref, thr, idx, n_sel, n_bits + 1)

    return (score > thr) | (eq & (idx < cut_ref[...]))


def _attn_prompt_kernel(q_ref, qi_ref, wi_ref, kk_ref, k_ref, v_ref, o_ref, score_ref, cut_ref, *, n_sel, row0):
    tq = q_ref.shape[0]
    s_len = v_ref.shape[0]
    n_bits = (s_len - 1).bit_length()
    q_pos = row0 + pl.program_id(1) * tq + lax.broadcasted_iota(I32, (tq, 1), 0)
    key_pos = lax.broadcasted_iota(I32, (tq, s_len), 1)
    causal = key_pos <= q_pos
    low = lax.broadcasted_iota(I32, (1, LANES), 1) < HEAD_DIM
    zero = jnp.zeros((), BF16)

    kk = kk_ref[...]
    score = jnp.zeros((tq, s_len), F32)
    for j in range(I_Q // LANES):
        blk = qi_ref[:, j * LANES:(j + 1) * LANES]
        for half in range(2):
            h = 2 * j + half
            qm = jnp.where(low if half == 0 else ~low, blk, zero)
            score = score + wi_ref[:, h:h + 1] * jnp.maximum(_dot(qm, kk), 0.0)
    score_ref[...] = jnp.where(causal, score, NEG_INF)
    sel = _select_mask(score_ref, key_pos, n_sel, n_bits, cut_ref) & causal
    bias = jnp.where(sel, 0.0, NEG_INF)

    kb = k_ref[...]
    vb = v_ref[...]
    for j in range(GROUP):
        blk = q_ref[:, j * LANES:(j + 1) * LANES]
        outs = []
        for half in range(2):
            qm = jnp.where(low if half == 0 else ~low, blk, zero)
            s = _dot(qm, kb) + bias
            m = jnp.max(s, axis=1, keepdims=True)
            p = jnp.exp(s - m)
            l = jnp.sum(p, axis=1, keepdims=True)
            outs.append(_dot(p.astype(BF16), vb) / l)
        o_ref[:, j * LANES:(j + 1) * LANES] = jnp.where(low, outs[0], outs[1]).astype(o_ref.dtype)


def _attn_prompt(q, qi, wi, kk_t, kb_t, vb, batch, tq):
    n_rows = q.shape[0]
    s_len = n_rows // batch
    assert s_len & (s_len - 1) == 0 and s_len % tq == 0
    n_sel = min(TOPK_MAX, s_len // 4)
    kv_step = min(KV_STEP, s_len)
    assert kv_step >= n_sel and kv_step % tq == 0 and s_len % kv_step == 0
    vb3 = vb.reshape(batch, s_len, A_KV)
    pieces = []
    for n_kv in range(kv_step, s_len + 1, kv_step):
        first = (n_kv - kv_step) // tq
        tile = lambda w, _f=first: pl.BlockSpec((tq, w), lambda b, i: (b * (s_len // tq) + _f + i, 0))
        keys_t = lambda w: pl.BlockSpec((None, w, n_kv), lambda b, i: (b, 0, 0))
        pieces.append(pl.pallas_call(
            functools.partial(_attn_prompt_kernel, n_sel=n_sel, row0=n_kv - kv_step),
            grid=(batch, kv_step // tq),
            in_specs=[tile(A_Q), tile(I_Q), tile(LANES), keys_t(LANES), keys_t(A_KV),
                      pl.BlockSpec((None, n_kv, A_KV), lambda b, i: (b, 0, 0))],
            out_specs=pl.BlockSpec((None, tq, A_Q), lambda b, i: (b, i, 0)),
            out_shape=jax.ShapeDtypeStruct((batch, kv_step, A_Q), BF16),
            scratch_shapes=[pltpu.VMEM((tq, n_kv), F32), pltpu.VMEM((tq, 1), I32)],
            compiler_params=pltpu.CompilerParams(dimension_semantics=("arbitrary", "arbitrary"),
                                                 vmem_limit_bytes=VMEM_LIMIT_BYTES),
            name=f"attn_prompt_{n_kv}",
        )(q, qi, wi, kk_t, kb_t, vb3))
    return jnp.concatenate(pieces, axis=1).reshape(n_rows, A_Q)


def _hgrn_consts(c):
    t = np.arange(c)
    tri = (t[None, :] <= t[:, None]).astype(np.float32)
    masks = []
    hs = c // 2
    while hs >= 1:
        blk = t // (2 * hs)
        right = (t % (2 * hs)) >= hs
        masks.append(((blk[:, None] == blk[None, :]) & right[:, None] & ~right[None, :]).astype(np.float32))
        hs //= 2
    return jnp.asarray(tri, BF16), jnp.asarray(np.stack(masks, 0), F32)


def _head_block_ones():
    d = np.arange(LANES) // HEAD_DIM
    return (d[:, None] == d[None, :]).astype(np.float32)


def _level_reference(b_ref, hs):
    c = b_ref.shape[0]
    row = lambda r, n: jnp.broadcast_to(b_ref[r:r + 1, :], (n, LANES))
    if 2 * hs >= SUBLANES:
        return jnp.concatenate([row(s + hs - 1, 2 * hs) for s in range(0, c, 2 * hs)], axis=0)
    sub = lax.broadcasted_iota(I32, (SUBLANES, 1), 0)
    tiles = []
    for s in range(0, c, SUBLANES):
        e = row(s + hs - 1, SUBLANES)
        for off in range(2 * hs, SUBLANES, 2 * hs):
            e = jnp.where(sub >= off, row(s + off + hs - 1, SUBLANES), e)
        tiles.append(e)
    return jnp.concatenate(tiles, axis=0)


def _hgrn_prompt_kernel(rq_ref, rf_ref, ri_ref, rg_ref, lb_ref, ng_ref, tri_ref, msk_ref, bo_ref,
                        o_ref, s_out_ref, st_ref, b_ref):
    c = tri_ref.shape[0]
    n_sub = rq_ref.shape[0] // c
    n_levels = msk_ref.shape[0]
    step = pl.program_id(1)

    @pl.when(step == 0)
    def _():
        st_ref[...] = jnp.zeros_like(st_ref)

    bo = bo_ref[...]
    bo_bf = bo.astype(BF16)
    same_head = bo > 0.5
    low = lax.broadcasted_iota(I32, (1, LANES), 1) < HEAD_DIM
    for p in range(rq_ref.shape[1] // LANES):
        sl = slice(p * LANES, (p + 1) * LANES)
        lb = lb_ref[:, sl]
        st = st_ref[p]
        for ci in range(n_sub):
            rows = slice(ci * c, (ci + 1) * c)
            f = lb + (1.0 - lb) * jax.nn.sigmoid(rf_ref[rows, sl])
            q = rq_ref[rows, sl]
            k = 1.0 - f
            v = ri_ref[rows, sl]
            vb = v.astype(BF16)

            b = _dot_exact_lhs(tri_ref[...], jnp.log(f))
            b_ref[p * n_sub + ci] = b
            o = _dot_nt((q * jnp.exp(b)).astype(BF16), st.astype(BF16))
            o = o + _dot((q * k).astype(BF16), bo_bf) * v

            a0 = jnp.zeros((c, c), F32)
            a1 = jnp.zeros((c, c), F32)
            for lv in range(n_levels):
                diff = b - _level_reference(b_ref.at[p * n_sub + ci], c >> (lv + 1))
                qdl = q * jnp.exp(jnp.minimum(diff, 0.0))
                kdl = (k * jnp.exp(jnp.minimum(-diff, 0.0))).astype(BF16)
                msk = msk_ref[lv] > 0.5
                a0 = a0 + jnp.where(msk, _dot_nt(jnp.where(low, qdl, 0.0).astype(BF16), kdl), 0.0)
                a1 = a1 + jnp.where(msk, _dot_nt(jnp.where(low, 0.0, qdl).astype(BF16), kdl), 0.0)
            o = o + jnp.where(low, _dot(a0.astype(BF16), vb), _dot(a1.astype(BF16), vb))

            b_last = b[c - 1:c, :]
            kdec = (k * jnp.exp(jnp.minimum(b_last - b, 0.0))).astype(BF16)
            st = st * jnp.exp(b_last) + jnp.where(same_head, _dot_tn(vb, kdec), 0.0)

            ms = _dot_exact_rhs(o * o, bo_bf) * (1.0 / HEAD_DIM)
            g = rg_ref[rows, sl]
            o_ref[rows, sl] = (o * lax.rsqrt(ms + RMS_EPS) * ng_ref[:, sl]
                               * (g * jax.nn.sigmoid(g))).astype(o_ref.dtype)
        st_ref[p] = st

        @pl.when(step == pl.num_programs(1) - 1)
        def _(p=p, st=st):
            s_out_ref[p] = st.T


def _hgrn_prompt(rq, rf, ri, rg, lb, norm_g, batch, chunk, rows_per_step):
    n_rows, width = rq.shape
    t_len = n_rows // batch
    assert chunk & (chunk - 1) == 0 and chunk >= SUBLANES
    assert rows_per_step % chunk == 0 and t_len % rows_per_step == 0
    nc = t_len // rows_per_step
    n_pairs = width // LANES
    tri, msk = _hgrn_consts(chunk)
    bo = jnp.asarray(_head_block_ones(), F32)
    tile = pl.BlockSpec((rows_per_step, width), lambda b, i: (b * nc + i, 0))
    const = lambda a: pl.BlockSpec(a.shape, lambda b, i, _nd=a.ndim: (0,) * _nd)
    lb2, ng2 = lb.reshape(1, -1), norm_g.reshape(1, -1)
    o, s_pairs = pl.pallas_call(
        _hgrn_prompt_kernel,
        grid=(batch, nc),
        in_specs=[tile, tile, tile, tile, const(lb2), const(ng2), const(tri), const(msk), const(bo)],
        out_specs=[tile, pl.BlockSpec((None, n_pairs, LANES, LANES), lambda b, i: (b, 0, 0, 0))],
        out_shape=[jax.ShapeDtypeStruct((n_rows, width), BF16),
                   jax.ShapeDtypeStruct((batch, n_pairs, LANES, LANES), F32)],
        scratch_shapes=[pltpu.VMEM((n_pairs, LANES, LANES), F32),
                        pltpu.VMEM((n_pairs * (rows_per_step // chunk), chunk, LANES), F32)],
        compiler_params=pltpu.CompilerParams(dimension_semantics=("arbitrary", "arbitrary"),
                                             vmem_limit_bytes=VMEM_LIMIT_BYTES),
        name="hgrn_prompt",
    )(rq, rf, ri, rg, lb2, ng2, tri, msk, bo)
    s0 = s_pairs[:, :, :HEAD_DIM, :HEAD_DIM]
    s1 = s_pairs[:, :, HEAD_DIM:, HEAD_DIM:]
    return o, jnp.stack([s0, s1], axis=2).reshape(batch, N_HEADS_R, HEAD_DIM, HEAD_DIM)


def _column(row, eye):
    return jnp.sum(jnp.where(eye, row, 0.0), axis=1, keepdims=True)


def _eye(n):
    return lax.broadcasted_iota(I32, (n, n), 0) == lax.broadcasted_iota(I32, (n, n), 1)


PAGE_UNROLL = 16


def _fetch_pages(pt_ref, b, n_pages, src_ref, dst_ref, sem):
    def start(p, carry):
        pltpu.make_async_copy(src_ref.at[pt_ref[b, p]], dst_ref.at[p], sem).start()
        return carry

    lax.fori_loop(0, n_pages, start, 0, unroll=PAGE_UNROLL)


def _wait_pages(n_pages, src_ref, dst_ref, sem):
    def wait(p, carry):
        pltpu.make_async_copy(src_ref.at[0], dst_ref.at[p], sem).wait()
        return carry

    lax.fori_loop(0, n_pages, wait, 0, unroll=PAGE_UNROLL)


def _prefetch_sequence(pt_ref, n_pages, streams):
    b = pl.program_id(0)
    slot = lax.rem(b, 2)

    @pl.when(b == 0)
    def _():
        for src, buf, sem in streams:
            _fetch_pages(pt_ref, 0, n_pages, src, buf.at[0], sem.at[0])

    @pl.when(b + 1 < pl.num_programs(0))
    def _():
        for src, buf, sem in streams:
            _fetch_pages(pt_ref, b + 1, n_pages, src, buf.at[1 - slot], sem.at[1 - slot])

    return slot


def _score_sample_kernel(pt_ref, qi_ref, wi_ref, kin_ref, cache_ref, o_ref, kbuf2, sem):
    b = pl.program_id(0)
    n_pages = kbuf2.shape[1]
    slot = _prefetch_sequence(pt_ref, n_pages, [(cache_ref, kbuf2, sem)])
    kbuf = kbuf2.at[slot]
    qi = qi_ref[...]
    wcol = _column(wi_ref[pl.ds(b, 1), :][:, :N_HEADS_IDX], _eye(N_HEADS_IDX))
    lane0 = lax.broadcasted_iota(I32, (1, PAGE_SIZE), 1) == 0
    ki_new = kin_ref[pl.ds(b, 1), :].astype(BF16).astype(F32)
    lg_new = jnp.sum(qi.astype(F32) * ki_new, axis=1, keepdims=True)
    sc_new = jnp.sum(wcol * jnp.maximum(lg_new, 0.0), axis=0, keepdims=True)
    o_ref[...] = jnp.full(o_ref.shape, NEG_INF, F32)
    o_ref[pl.ds(n_pages, 1), :] = jnp.where(lane0, sc_new, NEG_INF)
    _wait_pages(n_pages, cache_ref, kbuf, sem.at[slot])

    def page(p, carry):
        lg = _dot(qi, kbuf[p].astype(BF16))
        o_ref[pl.ds(p, 1), :] = jnp.sum(wcol * jnp.maximum(lg, 0.0), axis=0, keepdims=True)
        return carry

    lax.fori_loop(0, n_pages, page, 0, unroll=PAGE_UNROLL)


def _score_sample(page_table, qi3, wi, ki_new, cache_kiT, rows_pad):
    batch, n_pages = page_table.shape
    return pl.pallas_call(
        _score_sample_kernel,
        grid_spec=pltpu.PrefetchScalarGridSpec(
            num_scalar_prefetch=1,
            grid=(batch,),
            in_specs=[pl.BlockSpec((None, N_HEADS_IDX, HEAD_DIM), lambda b, pt: (b, 0, 0)),
                      pl.BlockSpec(wi.shape, lambda b, pt: (0, 0)),
                      pl.BlockSpec(ki_new.shape, lambda b, pt: (0, 0)),
                      pl.BlockSpec(memory_space=pl.ANY)],
            out_specs=pl.BlockSpec((None, rows_pad, PAGE_SIZE), lambda b, pt: (b, 0, 0)),
            scratch_shapes=[pltpu.VMEM((2, n_pages, HEAD_DIM, PAGE_SIZE), F32), pltpu.SemaphoreType.DMA((2,))]),
        out_shape=jax.ShapeDtypeStruct((batch, rows_pad, PAGE_SIZE), F32),
        compiler_params=pltpu.CompilerParams(dimension_semantics=("arbitrary",),
                                             vmem_limit_bytes=VMEM_LIMIT_BYTES),
        name="score_sample",
    )(page_table, qi3, wi, ki_new, cache_kiT)


def _select_sample_kernel(s_ref, o_ref, cut_ref, *, n_sel):
    rows, width = s_ref.shape
    n_bits = (width - 1).bit_length()
    idx = lax.broadcasted_iota(I32, (rows, width), 1)
    sel = _select_mask(s_ref, idx, n_sel, n_bits, cut_ref)
    o_ref[...] = jnp.where(sel, 0.0, NEG_INF)


def _select_sample(scores, n_sel):
    rows, width = scores.shape
    return pl.pallas_call(
        functools.partial(_select_sample_kernel, n_sel=n_sel),
        out_shape=jax.ShapeDtypeStruct((rows, width), F32),
        scratch_shapes=[pltpu.VMEM((rows, 1), I32)],
        compiler_params=pltpu.CompilerParams(vmem_limit_bytes=VMEM_LIMIT_BYTES),
        name="select_sample",
    )(scores)


def _attn_sample_kernel(pt_ref, q_ref, bias_ref, kn_ref, vn_ref, ck_ref, cv_ref, o_ref,
                        kbuf2, vbuf2, s_ref, ksem, vsem):
    b = pl.program_id(0)
    n_pages = kbuf2.shape[1]
    slot = _prefetch_sequence(pt_ref, n_pages, [(ck_ref, kbuf2, ksem), (cv_ref, vbuf2, vsem)])
    kbuf, vbuf = kbuf2.at[slot], vbuf2.at[slot]
    q = q_ref[...]
    lane_low = lax.broadcasted_iota(I32, (1, LANES), 1) < HEAD_DIM
    lane0 = lax.broadcasted_iota(I32, (1, PAGE_SIZE), 1) == 0
    kn = kn_ref[pl.ds(b, 1), :]
    vn = vn_ref[pl.ds(b, 1), :]
    s_new = jnp.sum(q.astype(F32) * kn.astype(BF16).astype(F32), axis=1, keepdims=True)
    s_ref[n_pages] = jnp.where(lane0, s_new, NEG_INF) + bias_ref[pl.ds(n_pages, 1), :]
    _wait_pages(n_pages, ck_ref, kbuf, ksem.at[slot])

    def qk(p, m):
        kt = kbuf[p].reshape(A_KV, PAGE_SIZE).astype(BF16)
        s = _dot(q, kt) + bias_ref[pl.ds(p, 1), :]
        s_ref[p] = s
        return jnp.maximum(m, s)

    m = lax.fori_loop(0, n_pages, qk, s_ref[n_pages], unroll=PAGE_UNROLL)
    m = jnp.max(m, axis=1, keepdims=True)
    _wait_pages(n_pages, cv_ref, vbuf, vsem.at[slot])

    def pv(p, carry):
        acc, l = carry
        e = jnp.exp(s_ref[p] - m)
        vt = vbuf[p].reshape(A_KV, PAGE_SIZE).astype(BF16)
        return acc + _dot_nt(e.astype(BF16), vt), l + e

    e_new = jnp.exp(s_ref[n_pages] - m)
    acc0 = jnp.sum(e_new, axis=1, keepdims=True).astype(BF16).astype(F32) * vn.astype(BF16).astype(F32)
    acc, l = lax.fori_loop(0, n_pages, pv, (acc0, e_new), unroll=PAGE_UNROLL)
    out = acc / jnp.sum(l, axis=1, keepdims=True)
    o_ref[...] = jnp.where(lane_low, out[:GROUP], out[GROUP:]).astype(o_ref.dtype)


def _attn_sample(page_table, q8, bias3, k_new, v_new, cache_kT, cache_vT):
    batch, n_pages = page_table.shape
    rows_pad = bias3.shape[1]
    return pl.pallas_call(
        _attn_sample_kernel,
        grid_spec=pltpu.PrefetchScalarGridSpec(
            num_scalar_prefetch=1,
            grid=(batch,),
            in_specs=[pl.BlockSpec((None, N_HEADS_A, LANES), lambda b, pt: (b, 0, 0)),
                      pl.BlockSpec((None, rows_pad, PAGE_SIZE), lambda b, pt: (b, 0, 0)),
                      pl.BlockSpec(k_new.shape, lambda b, pt: (0, 0)),
                      pl.BlockSpec(v_new.shape, lambda b, pt: (0, 0)),
                      pl.BlockSpec(memory_space=pl.ANY),
                      pl.BlockSpec(memory_space=pl.ANY)],
            out_specs=pl.BlockSpec((None, GROUP, LANES), lambda b, pt: (b, 0, 0)),
            scratch_shapes=[pltpu.VMEM((2, n_pages, N_KV_A, HEAD_DIM, PAGE_SIZE), F32),
                            pltpu.VMEM((2, n_pages, N_KV_A, HEAD_DIM, PAGE_SIZE), F32),
                            pltpu.VMEM((rows_pad, N_HEADS_A, PAGE_SIZE), F32),
                            pltpu.SemaphoreType.DMA((2,)), pltpu.SemaphoreType.DMA((2,))]),
        out_shape=jax.ShapeDtypeStruct((batch, GROUP, LANES), F32),
        compiler_params=pltpu.CompilerParams(dimension_semantics=("arbitrary",),
                                             vmem_limit_bytes=VMEM_LIMIT_BYTES),
        name="attn_sample",
    )(page_table, q8, bias3, k_new, v_new, cache_kT, cache_vT)


def _hgrn_sample_kernel(rq_ref, rf_ref, ri_ref, rg_ref, lb_ref, ng_ref, s_ref, o_ref, s_out_ref):
    b = pl.program_id(0)
    eye = _eye(HEAD_DIM)
    row = lambda ref: ref[pl.ds(b, 1), :]
    lb = lb_ref[...]
    f = lb + (1.0 - lb) * jax.nn.sigmoid(row(rf_ref))
    q, k, v, g = row(rq_ref), 1.0 - f, row(ri_ref), row(rg_ref)
    gate = ng_ref[...] * (g * jax.nn.sigmoid(g))
    outs = []
    for h in range(N_HEADS_R):
        sl = slice(h * HEAD_DIM, (h + 1) * HEAD_DIM)
        s_new = _column(f[:, sl], eye) * s_ref[h] + _column(k[:, sl], eye) * v[:, sl]
        s_out_ref[h] = s_new
        o = jnp.sum(_column(q[:, sl], eye) * s_new, axis=0, keepdims=True)
        outs.append(o * lax.rsqrt(jnp.mean(o * o, axis=1, keepdims=True) + RMS_EPS))
    o_ref[pl.ds(b, 1), :] = jnp.concatenate(outs, axis=1) * gate


def _hgrn_sample(rq, rf, ri, rg, lb, norm_g, state):
    batch = rq.shape[0]
    full = lambda a: pl.BlockSpec(a.shape, lambda b, _nd=a.ndim: (0,) * _nd)
    st = pl.BlockSpec((None, N_HEADS_R, HEAD_DIM, HEAD_DIM), lambda b: (b, 0, 0, 0))
    lb2, ng2 = lb.reshape(1, -1), norm_g.reshape(1, -1)
    return pl.pallas_call(
        _hgrn_sample_kernel,
        grid=(batch,),
        in_specs=[full(rq), full(rf), full(ri), full(rg), full(lb2), full(ng2), st],
        out_specs=[full(rq), st],
        out_shape=[jax.ShapeDtypeStruct(rq.shape, F32), jax.ShapeDtypeStruct(state.shape, F32)],
        compiler_params=pltpu.CompilerParams(dimension_semantics=("arbitrary",)),
        name="hgrn_sample",
    )(rq, rf, ri, rg, lb2, ng2, state)


FFN_CHUNK = 256
ROW_TILE = 512
Q_TILE = 256
KV_STEP = 256
HGRN_CHUNK = 128
HGRN_TILE = 256


def kernel(x_prompt, x_sample, cache_k, cache_v, cache_kidx, state_hgrn, page_table, w_in, w_out, hgrn_lb,
           hgrn_norm_g, ffn1_wg, ffn1_wu, ffn1_wd, ffn2_wg, ffn2_wu, ffn2_wd, ln1_g, ln1_b, ln2_g, ln2_b,
           ln3_g, ln3_b):
    assert w_in.shape[0] == DEPTH == 1
    batch, seq, d_model = x_prompt.shape
    dec_batch, dec_seq, _ = x_sample.shape
    assert dec_seq == 1
    n_pages = page_table.shape[1]
    l = 0

    lb = jnp.cumsum(jax.nn.softmax(hgrn_lb.astype(F32), axis=0), axis=0)[l]
    w1 = _ffn_weights(ffn1_wg[l], ffn1_wu[l], ffn1_wd[l])
    w2 = _ffn_weights(ffn2_wg[l], ffn2_wu[l], ffn2_wd[l])
    w_cat = _proj_weight(w_in[l])
    wo = w_out[l].astype(BF16)
    wo_a, wo_r = wo[:A_Q][_q_perm()], wo[A_Q:]

    xp = x_prompt.reshape(batch * seq, d_model)
    hp = _ffn(xp, w1, ln1_g[l], ln1_b[l], ROW_TILE)
    (q, k_t, v_t, kb_t, vb, qi, ki_t, kk_t, wi, rq, rf, ri, rg) = _proj(hp, w_cat, ROW_TILE, seq=seq)
    k = k_t.reshape(batch, N_KV_A, HEAD_DIM, seq).transpose(0, 3, 1, 2)
    v = v_t.reshape(batch, N_KV_A, HEAD_DIM, seq).transpose(0, 3, 1, 2)
    ki = ki_t.transpose(0, 2, 1)
    oa = _attn_prompt(q, qi, wi, kk_t, kb_t, vb, batch, Q_TILE)
    orr, sp = _hgrn_prompt(rq, rf, ri, rg, lb, hgrn_norm_g[l], batch, HGRN_CHUNK, HGRN_TILE)
    yp = _ffn(hp, w2, ln3_g[l], ln3_b[l], ROW_TILE, mix=(oa, orr, wo_a, wo_r, ln2_g[l], ln2_b[l]))

    xs = x_sample.reshape(dec_batch, d_model)
    hs = _ffn(xs, w1, ln1_g[l], ln1_b[l], dec_batch)
    (qs, ks, vs, _, _, qis, kis, _, wis, rqs, rfs, ris, rgs) = _proj(hs, w_cat, dec_batch)
    rows_pad = -(-(n_pages + 1) // SUBLANES) * SUBLANES
    cache_kiT = jnp.transpose(cache_kidx[l], (0, 2, 1))
    cache_kT = jnp.transpose(cache_k[l], (0, 2, 3, 1))
    cache_vT = jnp.transpose(cache_v[l], (0, 2, 3, 1))
    scores = _score_sample(page_table, qis.reshape(dec_batch, N_HEADS_IDX, HEAD_DIM), wis, kis, cache_kiT,
                           rows_pad)
    n_sel = min(TOPK_MAX, (n_pages * PAGE_SIZE + 1) // 4)
    bias = _select_sample(scores.reshape(dec_batch, rows_pad * PAGE_SIZE), n_sel)
    q_heads = qs.reshape(dec_batch, GROUP, 2, HEAD_DIM).transpose(0, 2, 1, 3).reshape(
        dec_batch, N_HEADS_A, HEAD_DIM)
    zeros = jnp.zeros_like(q_heads)
    q8 = jnp.concatenate([jnp.concatenate([q_heads[:, :GROUP], zeros[:, :GROUP]], -1),
                          jnp.concatenate([zeros[:, GROUP:], q_heads[:, GROUP:]], -1)], 1)
    oas = _attn_sample(page_table, q8, bias.reshape(dec_batch, rows_pad, PAGE_SIZE), ks, vs,
                       cache_kT, cache_vT).reshape(dec_batch, A_Q).astype(BF16)
    ors, ss = _hgrn_sample(rqs, rfs, ris, rgs, lb, hgrn_norm_g[l], state_hgrn[l])
    ys = _ffn(hs, w2, ln3_g[l], ln3_b[l], dec_batch,
              mix=(oas, ors.astype(BF16), wo_a, wo_r, ln2_g[l], ln2_b[l]))

    return (yp.reshape(batch, seq, d_model), ys.reshape(dec_batch, 1, d_model),
            k[None], v[None], ki[None], sp[None],
            ks.reshape(1, dec_batch, 1, N_KV_A, HEAD_DIM), vs.reshape(1, dec_batch, 1, N_KV_A, HEAD_DIM),
            kis.reshape(1, dec_batch, 1, HEAD_DIM), ss[None])
```

```python
import functools

import numpy as np
import jax
import jax.numpy as jnp
from jax import lax
from jax.experimental import pallas as pl
from jax.experimental.pallas import tpu as pltpu

F32 = jnp.float32
BF16 = jnp.bfloat16
I32 = jnp.int32

DEPTH = 1
N_HEADS_A = 8
N_KV_A = 2
HEAD_DIM = 64
N_HEADS_IDX = 8
N_HEADS_R = 8
TOPK_MAX = 256
PAGE_SIZE = 128
LN_EPS = 1e-5
RMS_EPS = 1e-6
DN_ALPHA = (2 * DEPTH) ** 0.25

LANES = 128
SUBLANES = 8
VMEM_LIMIT_BYTES = 56 * 1024 * 1024

NEG_INF = float("-inf")
INT_MIN = -(2 ** 31)


def _dot(a, b):
    return jnp.dot(a, b, preferred_element_type=F32)


def _dot_nt(a, b):
    return lax.dot_general(a, b, (((1,), (1,)), ((), ())), preferred_element_type=F32)


def _dot_tn(a, b):
    return lax.dot_general(a, b, (((0,), (0,)), ((), ())), preferred_element_type=F32)


def _split3(x):
    hi = x.astype(BF16)
    r1 = x - hi.astype(F32)
    mid = r1.astype(BF16)
    lo = (r1 - mid.astype(F32)).astype(BF16)
    return hi, mid, lo


def _dot_exact_lhs(m_bf, x):
    hi, mid, lo = _split3(x)
    return _dot(m_bf, hi) + _dot(m_bf, mid) + _dot(m_bf, lo)


def _dot_exact_rhs(x, m_bf):
    hi, mid, lo = _split3(x)
    return _dot(hi, m_bf) + _dot(mid, m_bf) + _dot(lo, m_bf)


def _layer_norm(x, g, b):
    mu = jnp.mean(x, -1, keepdims=True)
    xc = x - mu
    var = jnp.mean(xc * xc, -1, keepdims=True)
    return xc * lax.rsqrt(var + LN_EPS) * g + b


def _ffn_kernel(*refs, fc, with_mix):
    if with_mix:
        (h_ref, oa_ref, or_ref, woa_ref, wor_ref, g2_ref, b2_ref,
         wg_ref, wu_ref, wd_ref, g_ref, b_ref, o_ref, acc_ref, x_ref) = refs
        m = _dot(oa_ref[...], woa_ref[...]) + _dot(or_ref[...], wor_ref[...])
        x_ref[...] = _layer_norm(DN_ALPHA * h_ref[...] + m, g2_ref[...], b2_ref[...])
    else:
        x_ref, wg_ref, wu_ref, wd_ref, g_ref, b_ref, o_ref, acc_ref = refs
    xb = x_ref[...].astype(BF16)
    acc_ref[...] = jnp.zeros_like(acc_ref)
    for c0 in range(0, wg_ref.shape[1], fc):
        gg = _dot(xb, wg_ref[:, c0:c0 + fc])
        uu = _dot(xb, wu_ref[:, c0:c0 + fc])
        hh = (gg * jax.nn.sigmoid(gg) * uu).astype(BF16)
        acc_ref[...] += _dot(hh, wd_ref[c0:c0 + fc, :])
    o_ref[...] = _layer_norm(DN_ALPHA * x_ref[...] + 0.5 * acc_ref[...], g_ref[...], b_ref[...])


def _const_spec(shape):
    nd = len(shape)
    return pl.BlockSpec(shape, lambda i, _nd=nd: (0,) * _nd, pipeline_mode=pl.Buffered(1))


def _ffn_weights(wg, wu, wd):
    return wg.astype(BF16), wu.astype(BF16), wd.astype(BF16)


def _ffn(x, w3, g, b, tm, mix=None):
    wg3, wu3, wd3 = w3
    n_rows, d = x.shape
    assert n_rows % tm == 0 and wg3.shape[1] % FFN_CHUNK == 0
    row = lambda w: pl.BlockSpec((tm, w), lambda i: (i, 0))
    vec = lambda a: a.reshape(1, -1).astype(F32)
    args, specs = [x], [row(d)]
    scratch = [pltpu.VMEM((tm, d), F32)]
    if mix is not None:
        oa, orr, woa, wor, g2, b2 = mix
        args += [oa, orr, woa, wor, vec(g2), vec(b2)]
        specs += [row(oa.shape[1]), row(orr.shape[1]), _const_spec(woa.shape), _const_spec(wor.shape),
                  _const_spec((1, d)), _const_spec((1, d))]
        scratch.append(pltpu.VMEM((tm, d), F32))
    args += [wg3, wu3, wd3, vec(g), vec(b)]
    specs += [_const_spec(wg3.shape), _const_spec(wu3.shape), _const_spec(wd3.shape),
              _const_spec((1, d)), _const_spec((1, d))]
    return pl.pallas_call(
        functools.partial(_ffn_kernel, fc=FFN_CHUNK, with_mix=mix is not None),
        grid=(n_rows // tm,),
        in_specs=specs,
        out_specs=row(d),
        out_shape=jax.ShapeDtypeStruct((n_rows, d), F32),
        scratch_shapes=scratch,
        compiler_params=pltpu.CompilerParams(dimension_semantics=("arbitrary",),
                                             vmem_limit_bytes=VMEM_LIMIT_BYTES),
        name="ffn_mix" if mix is not None else "ffn",
    )(*args)


A_Q = N_HEADS_A * HEAD_DIM
A_KV = N_KV_A * HEAD_DIM
I_Q = N_HEADS_IDX * HEAD_DIM
R_W = N_HEADS_R * HEAD_DIM
_SPLIT = (A_Q, A_KV, A_KV, I_Q, HEAD_DIM, N_HEADS_IDX, R_W, R_W, R_W, R_W)
_OFF = tuple(int(v) for v in np.cumsum((0,) + _SPLIT))
GROUP = N_HEADS_A // N_KV_A


def _q_perm():
    idx = []
    for j in range(GROUP):
        idx += list(range(j * HEAD_DIM, (j + 1) * HEAD_DIM))
        idx += list(range((GROUP + j) * HEAD_DIM, (GROUP + j + 1) * HEAD_DIM))
    return np.asarray(idx, np.int32)


def _proj_weight(w_in):
    seg = lambda i: w_in[:, _OFF[i]:_OFF[i + 1]]
    q = seg(0)[:, _q_perm()]
    ki = seg(4)
    wi = jnp.pad(seg(5), ((0, 0), (0, LANES - N_HEADS_IDX)))
    cols = [q, seg(1), seg(2), seg(3), ki, ki, wi, seg(6), seg(7), seg(8), seg(9)]
    return jnp.concatenate(cols, axis=1).astype(BF16)


def _proj_kernel(h_ref, w_ref, q_ref, k_ref, v_ref, kb_ref, vb_ref, qi_ref, ki_ref, kk_ref, wi_ref,
                 rq_ref, rf_ref, ri_ref, rg_ref, *, keys_on_lanes):
    tr = (lambda a: a.T) if keys_on_lanes else (lambda a: a)
    hb = h_ref[...].astype(BF16)
    col = [0]

    def take(width):
        out = _dot(hb, w_ref[:, col[0]:col[0] + width])
        col[0] += width
        return out

    scale = HEAD_DIM ** -0.5
    q_ref[...] = (take(A_Q) * scale).astype(BF16)
    k = tr(take(A_KV))
    k_ref[...] = k
    kb_ref[...] = k.astype(BF16)
    v = take(A_KV)
    v_ref[...] = tr(v)
    vb_ref[...] = v.astype(BF16)
    qi_ref[...] = (take(I_Q) * scale).astype(BF16)
    kk = tr(take(LANES))
    ki_ref[...] = kk[:HEAD_DIM, :] if keys_on_lanes else kk[:, :HEAD_DIM]
    kk_ref[...] = kk.astype(BF16)
    wi_ref[...] = take(LANES) * (N_HEADS_IDX ** -0.5)
    rq_ref[...] = take(R_W)
    rf_ref[...] = take(R_W)
    ri_ref[...] = take(R_W)
    rg_ref[...] = take(R_W)


def _proj(h, w_cat, tm, seq=None):
    n_rows, d = h.shape
    assert n_rows % tm == 0
    row = lambda w: pl.BlockSpec((tm, w), lambda i: (i, 0))
    outs = [(A_Q, BF16), (A_KV, F32), (A_KV, F32), (A_KV, BF16), (A_KV, BF16), (I_Q, BF16),
            (HEAD_DIM, F32), (LANES, BF16), (LANES, F32), (R_W, F32), (R_W, F32), (R_W, F32), (R_W, F32)]
    out_specs = [row(w) for w, _ in outs]
    out_shape = [jax.ShapeDtypeStruct((n_rows, w), dt) for w, dt in outs]
    if seq is not None:
        assert seq % tm == 0 and n_rows % seq == 0
        per_seq = seq // tm
        for i in (1, 2, 3, 6, 7):
            w, dt = outs[i]
            out_specs[i] = pl.BlockSpec((None, w, tm), lambda i, _p=per_seq: (i // _p, 0, i % _p))
            out_shape[i] = jax.ShapeDtypeStruct((n_rows // seq, w, seq), dt)
    return pl.pallas_call(
        functools.partial(_proj_kernel, keys_on_lanes=seq is not None),
        grid=(n_rows // tm,),
        in_specs=[row(d), _const_spec(w_cat.shape)],
        out_specs=out_specs,
        out_shape=out_shape,
        compiler_params=pltpu.CompilerParams(dimension_semantics=("arbitrary",),
                                             vmem_limit_bytes=VMEM_LIMIT_BYTES),
        name="in_proj",
    )(h, w_cat)


def _ordered_float(code):
    key = code ^ jnp.int32(INT_MIN)
    bits = key ^ ((key >> 31) & jnp.int32(0x7FFFFFFF))
    f = pltpu.bitcast(bits, F32)
    return jnp.where(f != f, jnp.where(key >= 0, jnp.inf, NEG_INF), f)


def _count(mask):
    return jnp.sum(jnp.where(mask, 1.0, 0.0), axis=1, keepdims=True)


def _kth_largest(score_ref, n_sel):
    rows = score_ref.shape[0]

    def body(i, code):
        cand = code | jnp.left_shift(jnp.int32(1), 31 - i)
        cnt = _count(score_ref[...] >= _ordered_float(cand))
        return jnp.where(cnt >= n_sel, cand, code)

    return _ordered_float(lax.fori_loop(0, 32, body, jnp.zeros((rows, 1), I32)))


def _kth_largest_rows_on_lanes(score_t_ref, n_sel):
    n_keys, rows = score_t_ref.shape
    lanes_of_sums = 8
    assert n_keys % (SUBLANES * lanes_of_sums) == 0

    def body(i, code):
        cand = code | jnp.left_shift(jnp.int32(1), 31 - i)
        level = jnp.broadcast_to(_ordered_float(cand), (SUBLANES, rows))
        sums = [jnp.zeros((SUBLANES, rows), F32) for _ in range(lanes_of_sums)]
        for t in range(n_keys // SUBLANES):
            acc = sums[t % lanes_of_sums]
            tile = score_t_ref[t * SUBLANES:(t + 1) * SUBLANES, :]
            sums[t % lanes_of_sums] = jnp.where(tile >= level, acc + 1.0, acc)
        cnt = jnp.sum(functools.reduce(jnp.add, sums), axis=0, keepdims=True)
        return jnp.where(cnt >= n_sel, cand, code)

    return _ordered_float(lax.fori_loop(0, 32, body, jnp.zeros((1, rows), I32)))


def _tie_cut(score_ref, thr, idx, n_sel, n_bits):
    rows, width = score_ref.shape
    need = n_sel - _count(score_ref[...] > thr)

    def body(i, j):
        cand = j | jnp.left_shift(jnp.int32(1), n_bits - 1 - i)
        cnt = _count((score_ref[...] == thr) & (idx < cand))
        return jnp.where(cnt <= need, cand, j)

    return lax.fori_loop(0, n_bits, body, jnp.zeros((rows, 1), I32))


def _select_mask(score_ref, idx, n_sel, n_bits, cut_ref, thr=None):
    if thr is None:
        thr = _kth_largest(score_ref, n_sel)
    score = score_ref[...]
    eq = score == thr
    excess = ((_count(score > thr) + _count(eq)) > n_sel) & (thr != NEG_INF)
    cut_ref[...] = jnp.full(cut_ref.shape, 1 << n_bits, I32)

    @pl.when(jnp.max(jnp.where(excess, 1.0, 0.0)) > 0.0)
    def _():
        cut_ref[...] = _tie_cut(score_ref, thr, idx, n_sel, n_bits + 1)

    return (score > thr) | (eq & (idx < cut_ref[...]))


def _attn_prompt_kernel(q_ref, qi_ref, wi_ref, kk_ref, k_ref, v_ref, o_ref, score_ref, score_t_ref, cut_ref, *,
                        n_sel, row0):
    tq = q_ref.shape[0]
    s_len = v_ref.shape[0]
    n_bits = (s_len - 1).bit_length()
    q_pos = row0 + pl.program_id(1) * tq + lax.broadcasted_iota(I32, (tq, 1), 0)
    key_pos = lax.broadcasted_iota(I32, (tq, s_len), 1)
    causal = key_pos <= q_pos
    low = lax.broadcasted_iota(I32, (1, LANES), 1) < HEAD_DIM
    zero = jnp.zeros((), BF16)

    kk = kk_ref[...]
    score = jnp.zeros((tq, s_len), F32)
    for j in range(I_Q // LANES):
        blk = qi_ref[:, j * LANES:(j + 1) * LANES]
        for half in range(2):
            h = 2 * j + half
            qm = jnp.where(low if half == 0 else ~low, blk, zero)
            score = score + wi_ref[:, h:h + 1] * jnp.maximum(_dot(qm, kk), 0.0)
    score = jnp.where(causal, score, NEG_INF)
    score_ref[...] = score
    score_t_ref[...] = score.T
    thr = _column(_kth_largest_rows_on_lanes(score_t_ref, n_sel), _eye(tq))
    sel = _select_mask(score_ref, key_pos, n_sel, n_bits, cut_ref, thr=thr) & causal
    bias = jnp.where(sel, 0.0, NEG_INF)

    kb = k_ref[...]
    vb = v_ref[...]
    for j in range(GROUP):
        blk = q_ref[:, j * LANES:(j + 1) * LANES]
        outs = []
        for half in range(2):
            qm = jnp.where(low if half == 0 else ~low, blk, zero)
            s = _dot(qm, kb) + bias
            m = jnp.max(s, axis=1, keepdims=True)
            p = jnp.exp(s - m)
            l = jnp.sum(p, axis=1, keepdims=True)
            outs.append(_dot(p.astype(BF16), vb) / l)
        o_ref[:, j * LANES:(j + 1) * LANES] = jnp.where(low, outs[0], outs[1]).astype(o_ref.dtype)


def _attn_prompt(q, qi, wi, kk_t, kb_t, vb, batch, tq):
    n_rows = q.shape[0]
    s_len = n_rows // batch
    assert s_len & (s_len - 1) == 0 and s_len % tq == 0
    n_sel = min(TOPK_MAX, s_len // 4)
    kv_step = min(KV_STEP, s_len)
    assert kv_step >= n_sel and kv_step % tq == 0 and s_len % kv_step == 0
    vb3 = vb.reshape(batch, s_len, A_KV)
    pieces = []
    for n_kv in range(kv_step, s_len + 1, kv_step):
        first = (n_kv - kv_step) // tq
        tile = lambda w, _f=first: pl.BlockSpec((tq, w), lambda b, i: (b * (s_len // tq) + _f + i, 0))
        keys_t = lambda w: pl.BlockSpec((None, w, n_kv), lambda b, i: (b, 0, 0))
        pieces.append(pl.pallas_call(
            functools.partial(_attn_prompt_kernel, n_sel=n_sel, row0=n_kv - kv_step),
            grid=(batch, kv_step // tq),
            in_specs=[tile(A_Q), tile(I_Q), tile(LANES), keys_t(LANES), keys_t(A_KV),
                      pl.BlockSpec((None, n_kv, A_KV), lambda b, i: (b, 0, 0))],
            out_specs=pl.BlockSpec((None, tq, A_Q), lambda b, i: (b, i, 0)),
            out_shape=jax.ShapeDtypeStruct((batch, kv_step, A_Q), BF16),
            scratch_shapes=[pltpu.VMEM((tq, n_kv), F32), pltpu.VMEM((n_kv, tq), F32), pltpu.VMEM((tq, 1), I32)],
            compiler_params=pltpu.CompilerParams(dimension_semantics=("arbitrary", "arbitrary"),
                                                 vmem_limit_bytes=VMEM_LIMIT_BYTES),
            name=f"attn_prompt_{n_kv}",
        )(q, qi, wi, kk_t, kb_t, vb3))
    return jnp.concatenate(pieces, axis=1).reshape(n_rows, A_Q)


def _hgrn_consts(c):
    t = np.arange(c)
    tri = (t[None, :] <= t[:, None]).astype(np.float32)
    masks = []
    hs = c // 2
    while hs >= 1:
        blk = t // (2 * hs)
        right = (t % (2 * hs)) >= hs
        masks.append(((blk[:, None] == blk[None, :]) & right[:, None] & ~right[None, :]).astype(np.float32))
        hs //= 2
    return jnp.asarray(tri, BF16), jnp.asarray(np.stack(masks, 0), F32)


def _head_block_ones():
    d = np.arange(LANES) // HEAD_DIM
    return (d[:, None] == d[None, :]).astype(np.float32)


def _level_reference(b_ref, hs):
    c = b_ref.shape[0]
    row = lambda r, n: jnp.broadcast_to(b_ref[r:r + 1, :], (n, LANES))
    if 2 * hs >= SUBLANES:
        return jnp.concatenate([row(s + hs - 1, 2 * hs) for s in range(0, c, 2 * hs)], axis=0)
    sub = lax.broadcasted_iota(I32, (SUBLANES, 1), 0)
    tiles = []
    for s in range(0, c, SUBLANES):
        e = row(s + hs - 1, SUBLANES)
        for off in range(2 * hs, SUBLANES, 2 * hs):
            e = jnp.where(sub >= off, row(s + off + hs - 1, SUBLANES), e)
        tiles.append(e)
    return jnp.concatenate(tiles, axis=0)


def _hgrn_prompt_kernel(rq_ref, rf_ref, ri_ref, rg_ref, lb_ref, ng_ref, tri_ref, msk_ref, bo_ref,
                        o_ref, s_out_ref, st_ref, b_ref):
    c = tri_ref.shape[0]
    n_sub = rq_ref.shape[0] // c
    n_levels = msk_ref.shape[0]
    step = pl.program_id(1)

    @pl.when(step == 0)
    def _():
        st_ref[...] = jnp.zeros_like(st_ref)

    bo = bo_ref[...]
    bo_bf = bo.astype(BF16)
    same_head = bo > 0.5
    low = lax.broadcasted_iota(I32, (1, LANES), 1) < HEAD_DIM
    for p in range(rq_ref.shape[1] // LANES):
        sl = slice(p * LANES, (p + 1) * LANES)
        lb = lb_ref[:, sl]
        st = st_ref[p]
        for ci in range(n_sub):
            rows = slice(ci * c, (ci + 1) * c)
            f = lb + (1.0 - lb) * jax.nn.sigmoid(rf_ref[rows, sl])
            q = rq_ref[rows, sl]
            k = 1.0 - f
            v = ri_ref[rows, sl]
            vb = v.astype(BF16)

            b = _dot_exact_lhs(tri_ref[...], jnp.log(f))
            b_ref[p * n_sub + ci] = b
            o = _dot_nt((q * jnp.exp(b)).astype(BF16), st.astype(BF16))
            o = o + _dot((q * k).astype(BF16), bo_bf) * v

            a0 = jnp.zeros((c, c), F32)
            a1 = jnp.zeros((c, c), F32)
            for lv in range(n_levels):
                diff = b - _level_reference(b_ref.at[p * n_sub + ci], c >> (lv + 1))
                qdl = q * jnp.exp(jnp.minimum(diff, 0.0))
                kdl = (k * jnp.exp(jnp.minimum(-diff, 0.0))).astype(BF16)
                msk = msk_ref[lv] > 0.5
                a0 = a0 + jnp.where(msk, _dot_nt(jnp.where(low, qdl, 0.0).astype(BF16), kdl), 0.0)
                a1 = a1 + jnp.where(msk, _dot_nt(jnp.where(low, 0.0, qdl).astype(BF16), kdl), 0.0)
            o = o + jnp.where(low, _dot(a0.astype(BF16), vb), _dot(a1.astype(BF16), vb))

            b_last = b[c - 1:c, :]
            kdec = (k * jnp.exp(jnp.minimum(b_last - b, 0.0))).astype(BF16)
            st = st * jnp.exp(b_last) + jnp.where(same_head, _dot_tn(vb, kdec), 0.0)

            ms = _dot_exact_rhs(o * o, bo_bf) * (1.0 / HEAD_DIM)
            g = rg_ref[rows, sl]
            o_ref[rows, sl] = (o * lax.rsqrt(ms + RMS_EPS) * ng_ref[:, sl]
                               * (g * jax.nn.sigmoid(g))).astype(o_ref.dtype)
        st_ref[p] = st

        @pl.when(step == pl.num_programs(1) - 1)
        def _(p=p, st=st):
            s_out_ref[p] = st.T


def _hgrn_prompt(rq, rf, ri, rg, lb, norm_g, batch, chunk, rows_per_step):
    n_rows, width = rq.shape
    t_len = n_rows // batch
    assert chunk & (chunk - 1) == 0 and chunk >= SUBLANES
    assert rows_per_step % chunk == 0 and t_len % rows_per_step == 0
    nc = t_len // rows_per_step
    n_pairs = width // LANES
    tri, msk = _hgrn_consts(chunk)
    bo = jnp.asarray(_head_block_ones(), F32)
    tile = pl.BlockSpec((rows_per_step, width), lambda b, i: (b * nc + i, 0))
    const = lambda a: pl.BlockSpec(a.shape, lambda b, i, _nd=a.ndim: (0,) * _nd)
    lb2, ng2 = lb.reshape(1, -1), norm_g.reshape(1, -1)
    o, s_pairs = pl.pallas_call(
        _hgrn_prompt_kernel,
        grid=(batch, nc),
        in_specs=[tile, tile, tile, tile, const(lb2), const(ng2), const(tri), const(msk), const(bo)],
        out_specs=[tile, pl.BlockSpec((None, n_pairs, LANES, LANES), lambda b, i: (b, 0, 0, 0))],
        out_shape=[jax.ShapeDtypeStruct((n_rows, width), BF16),
                   jax.ShapeDtypeStruct((batch, n_pairs, LANES, LANES), F32)],
        scratch_shapes=[pltpu.VMEM((n_pairs, LANES, LANES), F32),
                        pltpu.VMEM((n_pairs * (rows_per_step // chunk), chunk, LANES), F32)],
        compiler_params=pltpu.CompilerParams(dimension_semantics=("arbitrary", "arbitrary"),
                                             vmem_limit_bytes=VMEM_LIMIT_BYTES),
        name="hgrn_prompt",
    )(rq, rf, ri, rg, lb2, ng2, tri, msk, bo)
    s0 = s_pairs[:, :, :HEAD_DIM, :HEAD_DIM]
    s1 = s_pairs[:, :, HEAD_DIM:, HEAD_DIM:]
    return o, jnp.stack([s0, s1], axis=2).reshape(batch, N_HEADS_R, HEAD_DIM, HEAD_DIM)


def _column(row, eye):
    return jnp.sum(jnp.where(eye, row, 0.0), axis=1, keepdims=True)


def _eye(n):
    return lax.broadcasted_iota(I32, (n, n), 0) == lax.broadcasted_iota(I32, (n, n), 1)


PAGE_UNROLL = 16


def _fetch_pages(pt_ref, b, n_pages, src_ref, dst_ref, sem):
    def start(p, carry):
        pltpu.make_async_copy(src_ref.at[pt_ref[b, p]], dst_ref.at[p], sem).start()
        return carry

    lax.fori_loop(0, n_pages, start, 0, unroll=PAGE_UNROLL)


def _wait_pages(n_pages, src_ref, dst_ref, sem):
    def wait(p, carry):
        pltpu.make_async_copy(src_ref.at[0], dst_ref.at[p], sem).wait()
        return carry

    lax.fori_loop(0, n_pages, wait, 0, unroll=PAGE_UNROLL)


def _prefetch_sequence(pt_ref, n_pages, streams):
    b = pl.program_id(0)
    slot = lax.rem(b, 2)

    @pl.when(b == 0)
    def _():
        for src, buf, sem in streams:
            _fetch_pages(pt_ref, 0, n_pages, src, buf.at[0], sem.at[0])

    @pl.when(b + 1 < pl.num_programs(0))
    def _():
        for src, buf, sem in streams:
            _fetch_pages(pt_ref, b + 1, n_pages, src, buf.at[1 - slot], sem.at[1 - slot])

    return slot


def _score_sample_kernel(pt_ref, qi_ref, wi_ref, kin_ref, cache_ref, o_ref, kbuf2, sem):
    b = pl.program_id(0)
    n_pages = kbuf2.shape[1]
    slot = _prefetch_sequence(pt_ref, n_pages, [(cache_ref, kbuf2, sem)])
    kbuf = kbuf2.at[slot]
    qi = qi_ref[...]
    wcol = _column(wi_ref[pl.ds(b, 1), :][:, :N_HEADS_IDX], _eye(N_HEADS_IDX))
    lane0 = lax.broadcasted_iota(I32, (1, PAGE_SIZE), 1) == 0
    ki_new = kin_ref[pl.ds(b, 1), :].astype(BF16).astype(F32)
    lg_new = jnp.sum(qi.astype(F32) * ki_new, axis=1, keepdims=True)
    sc_new = jnp.sum(wcol * jnp.maximum(lg_new, 0.0), axis=0, keepdims=True)
    o_ref[...] = jnp.full(o_ref.shape, NEG_INF, F32)
    o_ref[pl.ds(n_pages, 1), :] = jnp.where(lane0, sc_new, NEG_INF)
    _wait_pages(n_pages, cache_ref, kbuf, sem.at[slot])

    def page(p, carry):
        lg = _dot(qi, kbuf[p].astype(BF16))
        o_ref[pl.ds(p, 1), :] = jnp.sum(wcol * jnp.maximum(lg, 0.0), axis=0, keepdims=True)
        return carry

    lax.fori_loop(0, n_pages, page, 0, unroll=PAGE_UNROLL)


def _score_sample(page_table, qi3, wi, ki_new, cache_kiT, rows_pad):
    batch, n_pages = page_table.shape
    return pl.pallas_call(
        _score_sample_kernel,
        grid_spec=pltpu.PrefetchScalarGridSpec(
            num_scalar_prefetch=1,
            grid=(batch,),
            in_specs=[pl.BlockSpec((None, N_HEADS_IDX, HEAD_DIM), lambda b, pt: (b, 0, 0)),
                      pl.BlockSpec(wi.shape, lambda b, pt: (0, 0)),
                      pl.BlockSpec(ki_new.shape, lambda b, pt: (0, 0)),
                      pl.BlockSpec(memory_space=pl.ANY)],
            out_specs=pl.BlockSpec((None, rows_pad, PAGE_SIZE), lambda b, pt: (b, 0, 0)),
            scratch_shapes=[pltpu.VMEM((2, n_pages, HEAD_DIM, PAGE_SIZE), F32), pltpu.SemaphoreType.DMA((2,))]),
        out_shape=jax.ShapeDtypeStruct((batch, rows_pad, PAGE_SIZE), F32),
        compiler_params=pltpu.CompilerParams(dimension_semantics=("arbitrary",),
                                             vmem_limit_bytes=VMEM_LIMIT_BYTES),
        name="score_sample",
    )(page_table, qi3, wi, ki_new, cache_kiT)


def _select_sample_kernel(s_ref, o_ref, cut_ref, *, n_sel):
    rows, width = s_ref.shape
    n_bits = (width - 1).bit_length()
    idx = lax.broadcasted_iota(I32, (rows, width), 1)
    sel = _select_mask(s_ref, idx, n_sel, n_bits, cut_ref)
    o_ref[...] = jnp.where(sel, 0.0, NEG_INF)


def _select_sample(scores, n_sel):
    rows, width = scores.shape
    return pl.pallas_call(
        functools.partial(_select_sample_kernel, n_sel=n_sel),
        out_shape=jax.ShapeDtypeStruct((rows, width), F32),
        scratch_shapes=[pltpu.VMEM((rows, 1), I32)],
        compiler_params=pltpu.CompilerParams(vmem_limit_bytes=VMEM_LIMIT_BYTES),
        name="select_sample",
    )(scores)


def _attn_sample_kernel(pt_ref, q_ref, bias_ref, kn_ref, vn_ref, ck_ref, cv_ref, o_ref,
                        kbuf2, vbuf2, s_ref, ksem, vsem):
    b = pl.program_id(0)
    n_pages = kbuf2.shape[1]
    slot = _prefetch_sequence(pt_ref, n_pages, [(ck_ref, kbuf2, ksem), (cv_ref, vbuf2, vsem)])
    kbuf, vbuf = kbuf2.at[slot], vbuf2.at[slot]
    q = q_ref[...]
    lane_low = lax.broadcasted_iota(I32, (1, LANES), 1) < HEAD_DIM
    lane0 = lax.broadcasted_iota(I32, (1, PAGE_SIZE), 1) == 0
    kn = kn_ref[pl.ds(b, 1), :]
    vn = vn_ref[pl.ds(b, 1), :]
    s_new = jnp.sum(q.astype(F32) * kn.astype(BF16).astype(F32), axis=1, keepdims=True)
    s_ref[n_pages] = jnp.where(lane0, s_new, NEG_INF) + bias_ref[pl.ds(n_pages, 1), :]
    _wait_pages(n_pages, ck_ref, kbuf, ksem.at[slot])

    def qk(p, m):
        kt = kbuf[p].reshape(A_KV, PAGE_SIZE).astype(BF16)
        s = _dot(q, kt) + bias_ref[pl.ds(p, 1), :]
        s_ref[p] = s
        return jnp.maximum(m, s)

    m = lax.fori_loop(0, n_pages, qk, s_ref[n_pages], unroll=PAGE_UNROLL)
    m = jnp.max(m, axis=1, keepdims=True)
    _wait_pages(n_pages, cv_ref, vbuf, vsem.at[slot])

    def pv(p, carry):
        acc, l = carry
        e = jnp.exp(s_ref[p] - m)
        vt = vbuf[p].reshape(A_KV, PAGE_SIZE).astype(BF16)
        return acc + _dot_nt(e.astype(BF16), vt), l + e

    e_new = jnp.exp(s_ref[n_pages] - m)
    acc0 = jnp.sum(e_new, axis=1, keepdims=True).astype(BF16).astype(F32) * vn.astype(BF16).astype(F32)
    acc, l = lax.fori_loop(0, n_pages, pv, (acc0, e_new), unroll=PAGE_UNROLL)
    out = acc / jnp.sum(l, axis=1, keepdims=True)
    o_ref[...] = jnp.where(lane_low, out[:GROUP], out[GROUP:]).astype(o_ref.dtype)


def _attn_sample(page_table, q8, bias3, k_new, v_new, cache_kT, cache_vT):
    batch, n_pages = page_table.shape
    rows_pad = bias3.shape[1]
    return pl.pallas_call(
        _attn_sample_kernel,
        grid_spec=pltpu.PrefetchScalarGridSpec(
            num_scalar_prefetch=1,
            grid=(batch,),
            in_specs=[pl.BlockSpec((None, N_HEADS_A, LANES), lambda b, pt: (b, 0, 0)),
                      pl.BlockSpec((None, rows_pad, PAGE_SIZE), lambda b, pt: (b, 0, 0)),
                      pl.BlockSpec(k_new.shape, lambda b, pt: (0, 0)),
                      pl.BlockSpec(v_new.shape, lambda b, pt: (0, 0)),
                      pl.BlockSpec(memory_space=pl.ANY),
                      pl.BlockSpec(memory_space=pl.ANY)],
            out_specs=pl.BlockSpec((None, GROUP, LANES), lambda b, pt: (b, 0, 0)),
            scratch_shapes=[pltpu.VMEM((2, n_pages, N_KV_A, HEAD_DIM, PAGE_SIZE), F32),
                            pltpu.VMEM((2, n_pages, N_KV_A, HEAD_DIM, PAGE_SIZE), F32),
                            pltpu.VMEM((rows_pad, N_HEADS_A, PAGE_SIZE), F32),
                            pltpu.SemaphoreType.DMA((2,)), pltpu.SemaphoreType.DMA((2,))]),
        out_shape=jax.ShapeDtypeStruct((batch, GROUP, LANES), F32),
        compiler_params=pltpu.CompilerParams(dimension_semantics=("arbitrary",),
                                             vmem_limit_bytes=VMEM_LIMIT_BYTES),
        name="attn_sample",
    )(page_table, q8, bias3, k_new, v_new, cache_kT, cache_vT)


def _hgrn_sample_kernel(rq_ref, rf_ref, ri_ref, rg_ref, lb_ref, ng_ref, s_ref, o_ref, s_out_ref):
    eye = _eye(HEAD_DIM)
    lb = lb_ref[...]
    for j in range(s_ref.shape[0]):
        b = pl.program_id(0) * s_ref.shape[0] + j
        row = lambda ref: ref[pl.ds(b, 1), :]
        f = lb + (1.0 - lb) * jax.nn.sigmoid(row(rf_ref))
        q, k, v, g = row(rq_ref), 1.0 - f, row(ri_ref), row(rg_ref)
        gate = ng_ref[...] * (g * jax.nn.sigmoid(g))
        outs = []
        for h in range(N_HEADS_R):
            sl = slice(h * HEAD_DIM, (h + 1) * HEAD_DIM)
            s_new = _column(f[:, sl], eye) * s_ref[j, h] + _column(k[:, sl], eye) * v[:, sl]
            s_out_ref[j, h] = s_new
            o = jnp.sum(_column(q[:, sl], eye) * s_new, axis=0, keepdims=True)
            outs.append(o * lax.rsqrt(jnp.mean(o * o, axis=1, keepdims=True) + RMS_EPS))
        o_ref[pl.ds(b, 1), :] = jnp.concatenate(outs, axis=1) * gate


def _hgrn_sample(rq, rf, ri, rg, lb, norm_g, state):
    batch = rq.shape[0]
    per_step = HGRN_SAMPLE_GROUP if batch % HGRN_SAMPLE_GROUP == 0 else 1
    full = lambda a: pl.BlockSpec(a.shape, lambda b, _nd=a.ndim: (0,) * _nd)
    st = pl.BlockSpec((per_step, N_HEADS_R, HEAD_DIM, HEAD_DIM), lambda b: (b, 0, 0, 0))
    lb2, ng2 = lb.reshape(1, -1), norm_g.reshape(1, -1)
    return pl.pallas_call(
        _hgrn_sample_kernel,
        grid=(batch // per_step,),
        in_specs=[full(rq), full(rf), full(ri), full(rg), full(lb2), full(ng2), st],
        out_specs=[full(rq), st],
        out_shape=[jax.ShapeDtypeStruct(rq.shape, F32), jax.ShapeDtypeStruct(state.shape, F32)],
        compiler_params=pltpu.CompilerParams(dimension_semantics=("arbitrary",)),
        name="hgrn_sample",
    )(rq, rf, ri, rg, lb2, ng2, state)


FFN_CHUNK = 256
ROW_TILE = 512
Q_TILE = 256
KV_STEP = 256
HGRN_CHUNK = 128
HGRN_TILE = 512
HGRN_SAMPLE_GROUP = 1


def kernel(x_prompt, x_sample, cache_k, cache_v, cache_kidx, state_hgrn, page_table, w_in, w_out, hgrn_lb,
           hgrn_norm_g, ffn1_wg, ffn1_wu, ffn1_wd, ffn2_wg, ffn2_wu, ffn2_wd, ln1_g, ln1_b, ln2_g, ln2_b,
           ln3_g, ln3_b):
    assert w_in.shape[0] == DEPTH == 1
    batch, seq, d_model = x_prompt.shape
    dec_batch, dec_seq, _ = x_sample.shape
    assert dec_seq == 1
    n_pages = page_table.shape[1]
    l = 0

    lb = jnp.cumsum(jax.nn.softmax(hgrn_lb.astype(F32), axis=0), axis=0)[l]
    w1 = _ffn_weights(ffn1_wg[l], ffn1_wu[l], ffn1_wd[l])
    w2 = _ffn_weights(ffn2_wg[l], ffn2_wu[l], ffn2_wd[l])
    w_cat = _proj_weight(w_in[l])
    wo = w_out[l].astype(BF16)
    wo_a, wo_r = wo[:A_Q][_q_perm()], wo[A_Q:]

    xp = x_prompt.reshape(batch * seq, d_model)
    hp = _ffn(xp, w1, ln1_g[l], ln1_b[l], ROW_TILE)
    (q, k_t, v_t, kb_t, vb, qi, ki_t, kk_t, wi, rq, rf, ri, rg) = _proj(hp, w_cat, ROW_TILE, seq=seq)
    k = k_t.reshape(batch, N_KV_A, HEAD_DIM, seq).transpose(0, 3, 1, 2)
    v = v_t.reshape(batch, N_KV_A, HEAD_DIM, seq).transpose(0, 3, 1, 2)
    ki = ki_t.transpose(0, 2, 1)
    oa = _attn_prompt(q, qi, wi, kk_t, kb_t, vb, batch, Q_TILE)
    orr, sp = _hgrn_prompt(rq, rf, ri, rg, lb, hgrn_norm_g[l], batch, HGRN_CHUNK, HGRN_TILE)
    yp = _ffn(hp, w2, ln3_g[l], ln3_b[l], ROW_TILE, mix=(oa, orr, wo_a, wo_r, ln2_g[l], ln2_b[l]))

    xs = x_sample.reshape(dec_batch, d_model)
    hs = _ffn(xs, w1, ln1_g[l], ln1_b[l], dec_batch)
    (qs, ks, vs, _, _, qis, kis, _, wis, rqs, rfs, ris, rgs) = _proj(hs, w_cat, dec_batch)
    rows_pad = -(-(n_pages + 1) // SUBLANES) * SUBLANES
    cache_kiT = jnp.transpose(cache_kidx[l], (0, 2, 1))
    cache_kT = jnp.transpose(cache_k[l], (0, 2, 3, 1))
    cache_vT = jnp.transpose(cache_v[l], (0, 2, 3, 1))
    scores = _score_sample(page_table, qis.reshape(dec_batch, N_HEADS_IDX, HEAD_DIM), wis, kis, cache_kiT,
                           rows_pad)
    n_sel = min(TOPK_MAX, (n_pages * PAGE_SIZE + 1) // 4)
    bias = _select_sample(scores.reshape(dec_batch, rows_pad * PAGE_SIZE), n_sel)
    q_heads = qs.reshape(dec_batch, GROUP, 2, HEAD_DIM).transpose(0, 2, 1, 3).reshape(
        dec_batch, N_HEADS_A, HEAD_DIM)
    zeros = jnp.zeros_like(q_heads)
    q8 = jnp.concatenate([jnp.concatenate([q_heads[:, :GROUP], zeros[:, :GROUP]], -1),
                          jnp.concatenate([zeros[:, GROUP:], q_heads[:, GROUP:]], -1)], 1)
    oas = _attn_sample(page_table, q8, bias.reshape(dec_batch, rows_pad, PAGE_SIZE), ks, vs,
                       cache_kT, cache_vT).reshape(dec_batch, A_Q).astype(BF16)
    ors, ss = _hgrn_sample(rqs, rfs, ris, rgs, lb, hgrn_norm_g[l], state_hgrn[l])
    ys = _ffn(hs, w2, ln3_g[l], ln3_b[l], dec_batch,
              mix=(oas, ors.astype(BF16), wo_a, wo_r, ln2_g[l], ln2_b[l]))

    return (yp.reshape(batch, seq, d_model), ys.reshape(dec_batch, 1, d_model),
            k[None], v[None], ki[None], sp[None],
            ks.reshape(1, dec_batch, 1, N_KV_A, HEAD_DIM), vs.reshape(1, dec_batch, 1, N_KV_A, HEAD_DIM),
            kis.reshape(1, dec_batch, 1, HEAD_DIM), ss[None])
```

```python
import functools

import numpy as np
import jax
import jax.numpy as jnp
from jax import lax
from jax.experimental import pallas as pl
from jax.experimental.pallas import tpu as pltpu

F32 = jnp.float32
BF16 = jnp.bfloat16
I32 = jnp.int32

DEPTH = 1
N_HEADS_A = 8
N_KV_A = 2
HEAD_DIM = 64
N_HEADS_IDX = 8
N_HEADS_R = 8
TOPK_MAX = 256
PAGE_SIZE = 128
LN_EPS = 1e-5
RMS_EPS = 1e-6
DN_ALPHA = (2 * DEPTH) ** 0.25

LANES = 128
SUBLANES = 8
VMEM_LIMIT_BYTES = 56 * 1024 * 1024

NEG_INF = float("-inf")
INT_MIN = -(2 ** 31)


def _dot(a, b):
    return jnp.dot(a, b, preferred_element_type=F32)


def _dot_nt(a, b):
    return lax.dot_general(a, b, (((1,), (1,)), ((), ())), preferred_element_type=F32)


def _dot_tn(a, b):
    return lax.dot_general(a, b, (((0,), (0,)), ((), ())), preferred_element_type=F32)


def _split3(x):
    hi = x.astype(BF16)
    r1 = x - hi.astype(F32)
    mid = r1.astype(BF16)
    lo = (r1 - mid.astype(F32)).astype(BF16)
    return hi, mid, lo


def _dot_exact_lhs(m_bf, x):
    hi, mid, lo = _split3(x)
    return _dot(m_bf, hi) + _dot(m_bf, mid) + _dot(m_bf, lo)


def _dot_exact_rhs(x, m_bf):
    hi, mid, lo = _split3(x)
    return _dot(hi, m_bf) + _dot(mid, m_bf) + _dot(lo, m_bf)


def _layer_norm(x, g, b):
    mu = jnp.mean(x, -1, keepdims=True)
    xc = x - mu
    var = jnp.mean(xc * xc, -1, keepdims=True)
    return xc * lax.rsqrt(var + LN_EPS) * g + b


def _ffn_kernel(*refs, fc, with_mix):
    if with_mix:
        (h_ref, oa_ref, or_ref, woa_ref, wor_ref, g2_ref, b2_ref,
         wg_ref, wu_ref, wd_ref, g_ref, b_ref, o_ref, acc_ref, x_ref) = refs
        m = _dot(oa_ref[...], woa_ref[...]) + _dot(or_ref[...], wor_ref[...])
        x_ref[...] = _layer_norm(DN_ALPHA * h_ref[...] + m, g2_ref[...], b2_ref[...])
    else:
        x_ref, wg_ref, wu_ref, wd_ref, g_ref, b_ref, o_ref, acc_ref = refs
    xb = x_ref[...].astype(BF16)
    acc_ref[...] = jnp.zeros_like(acc_ref)
    for c0 in range(0, wg_ref.shape[1], fc):
        gg = _dot(xb, wg_ref[:, c0:c0 + fc])
        uu = _dot(xb, wu_ref[:, c0:c0 + fc])
        hh = (gg * jax.nn.sigmoid(gg) * uu).astype(BF16)
        acc_ref[...] += _dot(hh, wd_ref[c0:c0 + fc, :])
    o_ref[...] = _layer_norm(DN_ALPHA * x_ref[...] + 0.5 * acc_ref[...], g_ref[...], b_ref[...])


def _const_spec(shape):
    nd = len(shape)
    return pl.BlockSpec(shape, lambda i, _nd=nd: (0,) * _nd, pipeline_mode=pl.Buffered(1))


def _ffn_weights(wg, wu, wd):
    return wg.astype(BF16), wu.astype(BF16), wd.astype(BF16)


def _ffn(x, w3, g, b, tm, mix=None):
    wg3, wu3, wd3 = w3
    n_rows, d = x.shape
    assert n_rows % tm == 0 and wg3.shape[1] % FFN_CHUNK == 0
    row = lambda w: pl.BlockSpec((tm, w), lambda i: (i, 0))
    vec = lambda a: a.reshape(1, -1).astype(F32)
    args, specs = [x], [row(d)]
    scratch = [pltpu.VMEM((tm, d), F32)]
    if mix is not None:
        oa, orr, woa, wor, g2, b2 = mix
        args += [oa, orr, woa, wor, vec(g2), vec(b2)]
        specs += [row(oa.shape[1]), row(orr.shape[1]), _const_spec(woa.shape), _const_spec(wor.shape),
                  _const_spec((1, d)), _const_spec((1, d))]
        scratch.append(pltpu.VMEM((tm, d), F32))
    args += [wg3, wu3, wd3, vec(g), vec(b)]
    specs += [_const_spec(wg3.shape), _const_spec(wu3.shape), _const_spec(wd3.shape),
              _const_spec((1, d)), _const_spec((1, d))]
    return pl.pallas_call(
        functools.partial(_ffn_kernel, fc=FFN_CHUNK, with_mix=mix is not None),
        grid=(n_rows // tm,),
        in_specs=specs,
        out_specs=row(d),
        out_shape=jax.ShapeDtypeStruct((n_rows, d), F32),
        scratch_shapes=scratch,
        compiler_params=pltpu.CompilerParams(dimension_semantics=("arbitrary",),
                                             vmem_limit_bytes=VMEM_LIMIT_BYTES),
        name="ffn_mix" if mix is not None else "ffn",
    )(*args)


A_Q = N_HEADS_A * HEAD_DIM
A_KV = N_KV_A * HEAD_DIM
I_Q = N_HEADS_IDX * HEAD_DIM
R_W = N_HEADS_R * HEAD_DIM
_SPLIT = (A_Q, A_KV, A_KV, I_Q, HEAD_DIM, N_HEADS_IDX, R_W, R_W, R_W, R_W)
_OFF = tuple(int(v) for v in np.cumsum((0,) + _SPLIT))
GROUP = N_HEADS_A // N_KV_A


def _q_perm():
    idx = []
    for j in range(GROUP):
        idx += list(range(j * HEAD_DIM, (j + 1) * HEAD_DIM))
        idx += list(range((GROUP + j) * HEAD_DIM, (GROUP + j + 1) * HEAD_DIM))
    return np.asarray(idx, np.int32)


def _proj_weight(w_in):
    seg = lambda i: w_in[:, _OFF[i]:_OFF[i + 1]]
    q = seg(0)[:, _q_perm()]
    ki = seg(4)
    wi = jnp.pad(seg(5), ((0, 0), (0, LANES - N_HEADS_IDX)))
    cols = [q, seg(1), seg(2), seg(3), ki, ki, wi, seg(6), seg(7), seg(8), seg(9)]
    return jnp.concatenate(cols, axis=1).astype(BF16)


def _proj_kernel(h_ref, w_ref, q_ref, k_ref, v_ref, kb_ref, vb_ref, qi_ref, ki_ref, kk_ref, wi_ref,
                 rq_ref, rf_ref, ri_ref, rg_ref, *, keys_on_lanes):
    tr = (lambda a: a.T) if keys_on_lanes else (lambda a: a)
    hb = h_ref[...].astype(BF16)
    col = [0]

    def take(width):
        out = _dot(hb, w_ref[:, col[0]:col[0] + width])
        col[0] += width
        return out

    scale = HEAD_DIM ** -0.5
    q_ref[...] = (take(A_Q) * scale).astype(BF16)
    k = tr(take(A_KV))
    k_ref[...] = k
    kb_ref[...] = k.astype(BF16)
    v = take(A_KV)
    v_ref[...] = tr(v)
    vb_ref[...] = v.astype(BF16)
    qi_ref[...] = (take(I_Q) * scale).astype(BF16)
    kk = tr(take(LANES))
    ki_ref[...] = kk[:HEAD_DIM, :] if keys_on_lanes else kk[:, :HEAD_DIM]
    kk_ref[...] = kk.astype(BF16)
    wi_ref[...] = take(LANES) * (N_HEADS_IDX ** -0.5)
    rq_ref[...] = take(R_W)
    rf_ref[...] = take(R_W)
    ri_ref[...] = take(R_W)
    rg_ref[...] = take(R_W)


def _proj(h, w_cat, tm, seq=None):
    n_rows, d = h.shape
    assert n_rows % tm == 0
    row = lambda w: pl.BlockSpec((tm, w), lambda i: (i, 0))
    outs = [(A_Q, BF16), (A_KV, F32), (A_KV, F32), (A_KV, BF16), (A_KV, BF16), (I_Q, BF16),
            (HEAD_DIM, F32), (LANES, BF16), (LANES, F32), (R_W, F32), (R_W, F32), (R_W, F32), (R_W, F32)]
    out_specs = [row(w) for w, _ in outs]
    out_shape = [jax.ShapeDtypeStruct((n_rows, w), dt) for w, dt in outs]
    if seq is not None:
        assert seq % tm == 0 and n_rows % seq == 0
        per_seq = seq // tm
        for i in (1, 2, 3, 6, 7):
            w, dt = outs[i]
            out_specs[i] = pl.BlockSpec((None, w, tm), lambda i, _p=per_seq: (i // _p, 0, i % _p))
            out_shape[i] = jax.ShapeDtypeStruct((n_rows // seq, w, seq), dt)
    return pl.pallas_call(
        functools.partial(_proj_kernel, keys_on_lanes=seq is not None),
        grid=(n_rows // tm,),
        in_specs=[row(d), _const_spec(w_cat.shape)],
        out_specs=out_specs,
        out_shape=out_shape,
        compiler_params=pltpu.CompilerParams(dimension_semantics=("arbitrary",),
                                             vmem_limit_bytes=VMEM_LIMIT_BYTES),
        name="in_proj",
    )(h, w_cat)


def _ordered_float(code):
    key = code ^ jnp.int32(INT_MIN)
    bits = key ^ ((key >> 31) & jnp.int32(0x7FFFFFFF))
    f = pltpu.bitcast(bits, F32)
    return jnp.where(f != f, jnp.where(key >= 0, jnp.inf, NEG_INF), f)


def _count(mask):
    return jnp.sum(jnp.where(mask, 1.0, 0.0), axis=1, keepdims=True)


def _kth_largest(score_ref, n_sel):
    rows = score_ref.shape[0]

    def body(i, code):
        cand = code | jnp.left_shift(jnp.int32(1), 31 - i)
        cnt = _count(score_ref[...] >= _ordered_float(cand))
        return jnp.where(cnt >= n_sel, cand, code)

    return _ordered_float(lax.fori_loop(0, 32, body, jnp.zeros((rows, 1), I32)))


def _kth_largest_rows_on_lanes(score_t_ref, n_sel):
    n_keys, rows = score_t_ref.shape
    lanes_of_sums = 8
    assert n_keys % (SUBLANES * lanes_of_sums) == 0

    def body(i, code):
        cand = code | jnp.left_shift(jnp.int32(1), 31 - i)
        level = jnp.broadcast_to(_ordered_float(cand), (SUBLANES, rows))
        sums = [jnp.zeros((SUBLANES, rows), F32) for _ in range(lanes_of_sums)]
        for t in range(n_keys // SUBLANES):
            acc = sums[t % lanes_of_sums]
            tile = score_t_ref[t * SUBLANES:(t + 1) * SUBLANES, :]
            sums[t % lanes_of_sums] = jnp.where(tile >= level, acc + 1.0, acc)
        cnt = jnp.sum(functools.reduce(jnp.add, sums), axis=0, keepdims=True)
        return jnp.where(cnt >= n_sel, cand, code)

    return _ordered_float(lax.fori_loop(0, 32, body, jnp.zeros((1, rows), I32)))


def _tie_cut(score_ref, thr, idx, n_sel, n_bits):
    rows, width = score_ref.shape
    need = n_sel - _count(score_ref[...] > thr)

    def body(i, j):
        cand = j | jnp.left_shift(jnp.int32(1), n_bits - 1 - i)
        cnt = _count((score_ref[...] == thr) & (idx < cand))
        return jnp.where(cnt <= need, cand, j)

    return lax.fori_loop(0, n_bits, body, jnp.zeros((rows, 1), I32))


def _select_mask(score_ref, idx, n_sel, n_bits, cut_ref, thr=None):
    if thr is None:
        thr = _kth_largest(score_ref, n_sel)
    score = score_ref[...]
    eq = score == thr
    excess = ((_count(score > thr) + _count(eq)) > n_sel) & (thr != NEG_INF)
    cut_ref[...] = jnp.full(cut_ref.shape, 1 << n_bits, I32)

    @pl.when(jnp.max(jnp.where(excess, 1.0, 0.0)) > 0.0)
    def _():
        cut_ref[...] = _tie_cut(score_ref, thr, idx, n_sel, n_bits + 1)

    return (score > thr) | (eq & (idx < cut_ref[...]))


def _attn_prompt_kernel(q_ref, qi_ref, wi_ref, kk_ref, k_ref, v_ref, o_ref, score_ref, score_t_ref, cut_ref, *,
                        n_sel, row0):
    tq = q_ref.shape[0]
    s_len = v_ref.shape[0]
    n_bits = (s_len - 1).bit_length()
    q_pos = row0 + pl.program_id(1) * tq + lax.broadcasted_iota(I32, (tq, 1), 0)
    key_pos = lax.broadcasted_iota(I32, (tq, s_len), 1)
    causal = key_pos <= q_pos
    low = lax.broadcasted_iota(I32, (1, LANES), 1) < HEAD_DIM
    zero = jnp.zeros((), BF16)

    kk = kk_ref[...]
    score = jnp.zeros((tq, s_len), F32)
    for j in range(I_Q // LANES):
        blk = qi_ref[:, j * LANES:(j + 1) * LANES]
        for half in range(2):
            h = 2 * j + half
            qm = jnp.where(low if half == 0 else ~low, blk, zero)
            score = score + wi_ref[:, h:h + 1] * jnp.maximum(_dot(qm, kk), 0.0)
    score = jnp.where(causal, score, NEG_INF)
    score_ref[...] = score
    score_t_ref[...] = score.T
    thr = _column(_kth_largest_rows_on_lanes(score_t_ref, n_sel), _eye(tq))
    sel = _select_mask(score_ref, key_pos, n_sel, n_bits, cut_ref, thr=thr) & causal
    bias = jnp.where(sel, 0.0, NEG_INF)

    kb = k_ref[...]
    vb = v_ref[...]
    for j in range(GROUP):
        blk = q_ref[:, j * LANES:(j + 1) * LANES]
        outs = []
        for half in range(2):
            qm = jnp.where(low if half == 0 else ~low, blk, zero)
            s = _dot(qm, kb) + bias
            m = jnp.max(s, axis=1, keepdims=True)
            p = jnp.exp(s - m)
            l = jnp.sum(p, axis=1, keepdims=True)
            outs.append(_dot(p.astype(BF16), vb) / l)
        o_ref[:, j * LANES:(j + 1) * LANES] = jnp.where(low, outs[0], outs[1]).astype(o_ref.dtype)


def _attn_prompt(q, qi, wi, kk_t, kb_t, vb, batch, tq):
    n_rows = q.shape[0]
    s_len = n_rows // batch
    assert s_len & (s_len - 1) == 0 and s_len % tq == 0
    n_sel = min(TOPK_MAX, s_len // 4)
    kv_step = min(KV_STEP, s_len)
    assert kv_step >= n_sel and kv_step % tq == 0 and s_len % kv_step == 0
    vb3 = vb.reshape(batch, s_len, A_KV)
    pieces = []
    for n_kv in range(kv_step, s_len + 1, kv_step):
        first = (n_kv - kv_step) // tq
        tile = lambda w, _f=first: pl.BlockSpec((tq, w), lambda b, i: (b * (s_len // tq) + _f + i, 0))
        keys_t = lambda w: pl.BlockSpec((None, w, n_kv), lambda b, i: (b, 0, 0))
        pieces.append(pl.pallas_call(
            functools.partial(_attn_prompt_kernel, n_sel=n_sel, row0=n_kv - kv_step),
            grid=(batch, kv_step // tq),
            in_specs=[tile(A_Q), tile(I_Q), tile(LANES), keys_t(LANES), keys_t(A_KV),
                      pl.BlockSpec((None, n_kv, A_KV), lambda b, i: (b, 0, 0))],
            out_specs=pl.BlockSpec((None, tq, A_Q), lambda b, i: (b, i, 0)),
            out_shape=jax.ShapeDtypeStruct((batch, kv_step, A_Q), BF16),
            scratch_shapes=[pltpu.VMEM((tq, n_kv), F32), pltpu.VMEM((n_kv, tq), F32), pltpu.VMEM((tq, 1), I32)],
            compiler_params=pltpu.CompilerParams(dimension_semantics=("arbitrary", "arbitrary"),
                                                 vmem_limit_bytes=VMEM_LIMIT_BYTES),
            name=f"attn_prompt_{n_kv}",
        )(q, qi, wi, kk_t, kb_t, vb3))
    return jnp.concatenate(pieces, axis=1).reshape(n_rows, A_Q)


def _hgrn_consts(c):
    t = np.arange(c)
    tri = (t[None, :] <= t[:, None]).astype(np.float32)
    masks = []
    hs = c // 2
    while hs >= 1:
        blk = t // (2 * hs)
        right = (t % (2 * hs)) >= hs
        masks.append(((blk[:, None] == blk[None, :]) & right[:, None] & ~right[None, :]).astype(np.float32))
        hs //= 2
    return jnp.asarray(tri, BF16), jnp.asarray(np.stack(masks, 0), F32)


def _head_block_ones():
    d = np.arange(LANES) // HEAD_DIM
    return (d[:, None] == d[None, :]).astype(np.float32)


def _level_reference(b_ref, hs):
    c = b_ref.shape[0]
    row = lambda r, n: jnp.broadcast_to(b_ref[r:r + 1, :], (n, LANES))
    if 2 * hs >= SUBLANES:
        return jnp.concatenate([row(s + hs - 1, 2 * hs) for s in range(0, c, 2 * hs)], axis=0)
    sub = lax.broadcasted_iota(I32, (SUBLANES, 1), 0)
    tiles = []
    for s in range(0, c, SUBLANES):
        e = row(s + hs - 1, SUBLANES)
        for off in range(2 * hs, SUBLANES, 2 * hs):
            e = jnp.where(sub >= off, row(s + off + hs - 1, SUBLANES), e)
        tiles.append(e)
    return jnp.concatenate(tiles, axis=0)


def _hgrn_prompt_kernel(rq_ref, rf_ref, ri_ref, rg_ref, lb_ref, ng_ref, tri_ref, msk_ref, bo_ref,
                        o_ref, s_out_ref, st_ref, b_ref):
    c = tri_ref.shape[0]
    n_sub = rq_ref.shape[0] // c
    n_levels = msk_ref.shape[0]
    step = pl.program_id(1)

    @pl.when(step == 0)
    def _():
        st_ref[...] = jnp.zeros_like(st_ref)

    bo = bo_ref[...]
    bo_bf = bo.astype(BF16)
    same_head = bo > 0.5
    low = lax.broadcasted_iota(I32, (1, LANES), 1) < HEAD_DIM
    for p in range(rq_ref.shape[1] // LANES):
        sl = slice(p * LANES, (p + 1) * LANES)
        lb = lb_ref[:, sl]
        st = st_ref[p]
        for ci in range(n_sub):
            rows = slice(ci * c, (ci + 1) * c)
            f = lb + (1.0 - lb) * jax.nn.sigmoid(rf_ref[rows, sl])
            q = rq_ref[rows, sl]
            k = 1.0 - f
            v = ri_ref[rows, sl]
            vb = v.astype(BF16)

            b = _dot_exact_lhs(tri_ref[...], jnp.log(f))
            b_ref[p * n_sub + ci] = b
            o = _dot_nt((q * jnp.exp(b)).astype(BF16), st.astype(BF16))
            o = o + _dot((q * k).astype(BF16), bo_bf) * v

            a0 = jnp.zeros((c, c), F32)
            a1 = jnp.zeros((c, c), F32)
            for lv in range(n_levels):
                diff = b - _level_reference(b_ref.at[p * n_sub + ci], c >> (lv + 1))
                qdl = q * jnp.exp(jnp.minimum(diff, 0.0))
                kdl = (k * jnp.exp(jnp.minimum(-diff, 0.0))).astype(BF16)
                msk = msk_ref[lv] > 0.5
                a0 = a0 + jnp.where(msk, _dot_nt(jnp.where(low, qdl, 0.0).astype(BF16), kdl), 0.0)
                a1 = a1 + jnp.where(msk, _dot_nt(jnp.where(low, 0.0, qdl).astype(BF16), kdl), 0.0)
            o = o + jnp.where(low, _dot(a0.astype(BF16), vb), _dot(a1.astype(BF16), vb))

            b_last = b[c - 1:c, :]
            kdec = (k * jnp.exp(jnp.minimum(b_last - b, 0.0))).astype(BF16)
            st = st * jnp.exp(b_last) + jnp.where(same_head, _dot_tn(vb, kdec), 0.0)

            ms = _dot_exact_rhs(o * o, bo_bf) * (1.0 / HEAD_DIM)
            g = rg_ref[rows, sl]
            o_ref[rows, sl] = (o * lax.rsqrt(ms + RMS_EPS) * ng_ref[:, sl]
                               * (g * jax.nn.sigmoid(g))).astype(o_ref.dtype)
        st_ref[p] = st

        @pl.when(step == pl.num_programs(1) - 1)
        def _(p=p, st=st):
            s_out_ref[p] = st.T


def _hgrn_prompt(rq, rf, ri, rg, lb, norm_g, batch, chunk, rows_per_step):
    n_rows, width = rq.shape
    t_len = n_rows // batch
    assert chunk & (chunk - 1) == 0 and chunk >= SUBLANES
    assert rows_per_step % chunk == 0 and t_len % rows_per_step == 0
    nc = t_len // rows_per_step
    n_pairs = width // LANES
    tri, msk = _hgrn_consts(chunk)
    bo = jnp.asarray(_head_block_ones(), F32)
    tile = pl.BlockSpec((rows_per_step, width), lambda b, i: (b * nc + i, 0))
    const = lambda a: pl.BlockSpec(a.shape, lambda b, i, _nd=a.ndim: (0,) * _nd)
    lb2, ng2 = lb.reshape(1, -1), norm_g.reshape(1, -1)
    o, s_pairs = pl.pallas_call(
        _hgrn_prompt_kernel,
        grid=(batch, nc),
        in_specs=[tile, tile, tile, tile, const(lb2), const(ng2), const(tri), const(msk), const(bo)],
        out_specs=[tile, pl.BlockSpec((None, n_pairs, LANES, LANES), lambda b, i: (b, 0, 0, 0))],
        out_shape=[jax.ShapeDtypeStruct((n_rows, width), BF16),
                   jax.ShapeDtypeStruct((batch, n_pairs, LANES, LANES), F32)],
        scratch_shapes=[pltpu.VMEM((n_pairs, LANES, LANES), F32),
                        pltpu.VMEM((n_pairs * (rows_per_step // chunk), chunk, LANES), F32)],
        compiler_params=pltpu.CompilerParams(dimension_semantics=("arbitrary", "arbitrary"),
                                             vmem_limit_bytes=VMEM_LIMIT_BYTES),
        name="hgrn_prompt",
    )(rq, rf, ri, rg, lb2, ng2, tri, msk, bo)
    s0 = s_pairs[:, :, :HEAD_DIM, :HEAD_DIM]
    s1 = s_pairs[:, :, HEAD_DIM:, HEAD_DIM:]
    return o, jnp.stack([s0, s1], axis=2).reshape(batch, N_HEADS_R, HEAD_DIM, HEAD_DIM)


def _column(row, eye):
    return jnp.sum(jnp.where(eye, row, 0.0), axis=1, keepdims=True)


def _eye(n):
    return lax.broadcasted_iota(I32, (n, n), 0) == lax.broadcasted_iota(I32, (n, n), 1)


PAGE_UNROLL = 32


def _fetch_pages(pt_ref, b, n_pages, src_ref, dst_ref, sem):
    def start(p, carry):
        pltpu.make_async_copy(src_ref.at[pt_ref[b, p]], dst_ref.at[p], sem).start()
        return carry

    lax.fori_loop(0, n_pages, start, 0, unroll=PAGE_UNROLL)


def _wait_pages(n_pages, src_ref, dst_ref, sem):
    def wait(p, carry):
        pltpu.make_async_copy(src_ref.at[0], dst_ref.at[p], sem).wait()
        return carry

    lax.fori_loop(0, n_pages, wait, 0, unroll=PAGE_UNROLL)


def _prefetch_sequence(pt_ref, n_pages, streams):
    b = pl.program_id(0)
    slot = lax.rem(b, 2)

    @pl.when(b == 0)
    def _():
        for src, buf, sem in streams:
            _fetch_pages(pt_ref, 0, n_pages, src, buf.at[0], sem.at[0])

    @pl.when(b + 1 < pl.num_programs(0))
    def _():
        for src, buf, sem in streams:
            _fetch_pages(pt_ref, b + 1, n_pages, src, buf.at[1 - slot], sem.at[1 - slot])

    return slot


def _score_sample_kernel(pt_ref, qi_ref, wi_ref, kin_ref, cache_ref, o_ref, kbuf2, sem):
    b = pl.program_id(0)
    n_pages = kbuf2.shape[1]
    slot = _prefetch_sequence(pt_ref, n_pages, [(cache_ref, kbuf2, sem)])
    kbuf = kbuf2.at[slot]
    qi = qi_ref[...]
    wcol = _column(wi_ref[pl.ds(b, 1), :][:, :N_HEADS_IDX], _eye(N_HEADS_IDX))
    lane0 = lax.broadcasted_iota(I32, (1, PAGE_SIZE), 1) == 0
    ki_new = kin_ref[pl.ds(b, 1), :].astype(BF16).astype(F32)
    lg_new = jnp.sum(qi.astype(F32) * ki_new, axis=1, keepdims=True)
    sc_new = jnp.sum(wcol * jnp.maximum(lg_new, 0.0), axis=0, keepdims=True)
    o_ref[...] = jnp.full(o_ref.shape, NEG_INF, F32)
    o_ref[pl.ds(n_pages, 1), :] = jnp.where(lane0, sc_new, NEG_INF)
    _wait_pages(n_pages, cache_ref, kbuf, sem.at[slot])

    def page(p, carry):
        lg = _dot(qi, kbuf[p].astype(BF16))
        o_ref[pl.ds(p, 1), :] = jnp.sum(wcol * jnp.maximum(lg, 0.0), axis=0, keepdims=True)
        return carry

    lax.fori_loop(0, n_pages, page, 0, unroll=PAGE_UNROLL)


def _score_sample(page_table, qi3, wi, ki_new, cache_kiT, rows_pad):
    batch, n_pages = page_table.shape
    return pl.pallas_call(
        _score_sample_kernel,
        grid_spec=pltpu.PrefetchScalarGridSpec(
            num_scalar_prefetch=1,
            grid=(batch,),
            in_specs=[pl.BlockSpec((None, N_HEADS_IDX, HEAD_DIM), lambda b, pt: (b, 0, 0)),
                      pl.BlockSpec(wi.shape, lambda b, pt: (0, 0)),
                      pl.BlockSpec(ki_new.shape, lambda b, pt: (0, 0)),
                      pl.BlockSpec(memory_space=pl.ANY)],
            out_specs=pl.BlockSpec((None, rows_pad, PAGE_SIZE), lambda b, pt: (b, 0, 0)),
            scratch_shapes=[pltpu.VMEM((2, n_pages, HEAD_DIM, PAGE_SIZE), F32), pltpu.SemaphoreType.DMA((2,))]),
        out_shape=jax.ShapeDtypeStruct((batch, rows_pad, PAGE_SIZE), F32),
        compiler_params=pltpu.CompilerParams(dimension_semantics=("arbitrary",),
                                             vmem_limit_bytes=VMEM_LIMIT_BYTES),
        name="score_sample",
    )(page_table, qi3, wi, ki_new, cache_kiT)


def _select_sample_kernel(s_ref, o_ref, cut_ref, *, n_sel):
    rows, width = s_ref.shape
    n_bits = (width - 1).bit_length()
    idx = lax.broadcasted_iota(I32, (rows, width), 1)
    sel = _select_mask(s_ref, idx, n_sel, n_bits, cut_ref)
    o_ref[...] = jnp.where(sel, 0.0, NEG_INF)


def _select_sample(scores, n_sel):
    rows, width = scores.shape
    return pl.pallas_call(
        functools.partial(_select_sample_kernel, n_sel=n_sel),
        out_shape=jax.ShapeDtypeStruct((rows, width), F32),
        scratch_shapes=[pltpu.VMEM((rows, 1), I32)],
        compiler_params=pltpu.CompilerParams(vmem_limit_bytes=VMEM_LIMIT_BYTES),
        name="select_sample",
    )(scores)


def _attn_sample_kernel(pt_ref, q_ref, bias_ref, kn_ref, vn_ref, ck_ref, cv_ref, o_ref,
                        kbuf2, vbuf2, s_ref, ksem, vsem):
    b = pl.program_id(0)
    n_pages = kbuf2.shape[1]
    slot = _prefetch_sequence(pt_ref, n_pages, [(ck_ref, kbuf2, ksem), (cv_ref, vbuf2, vsem)])
    kbuf, vbuf = kbuf2.at[slot], vbuf2.at[slot]
    q = q_ref[...]
    lane_low = lax.broadcasted_iota(I32, (1, LANES), 1) < HEAD_DIM
    lane0 = lax.broadcasted_iota(I32, (1, PAGE_SIZE), 1) == 0
    kn = kn_ref[pl.ds(b, 1), :]
    vn = vn_ref[pl.ds(b, 1), :]
    s_new = jnp.sum(q.astype(F32) * kn.astype(BF16).astype(F32), axis=1, keepdims=True)
    s_ref[n_pages] = jnp.where(lane0, s_new, NEG_INF) + bias_ref[pl.ds(n_pages, 1), :]
    _wait_pages(n_pages, ck_ref, kbuf, ksem.at[slot])

    def qk(p, m):
        kt = kbuf[p].reshape(A_KV, PAGE_SIZE).astype(BF16)
        s = _dot(q, kt) + bias_ref[pl.ds(p, 1), :]
        s_ref[p] = s
        return jnp.maximum(m, s)

    m = lax.fori_loop(0, n_pages, qk, s_ref[n_pages], unroll=PAGE_UNROLL)
    m = jnp.max(m, axis=1, keepdims=True)
    _wait_pages(n_pages, cv_ref, vbuf, vsem.at[slot])

    def pv(p, carry):
        acc, l = carry
        e = jnp.exp(s_ref[p] - m)
        vt = vbuf[p].reshape(A_KV, PAGE_SIZE).astype(BF16)
        return acc + _dot_nt(e.astype(BF16), vt), l + e

    e_new = jnp.exp(s_ref[n_pages] - m)
    acc0 = jnp.sum(e_new, axis=1, keepdims=True).astype(BF16).astype(F32) * vn.astype(BF16).astype(F32)
    acc, l = lax.fori_loop(0, n_pages, pv, (acc0, e_new), unroll=PAGE_UNROLL)
    out = acc / jnp.sum(l, axis=1, keepdims=True)
    o_ref[...] = jnp.where(lane_low, out[:GROUP], out[GROUP:]).astype(o_ref.dtype)


def _attn_sample(page_table, q8, bias3, k_new, v_new, cache_kT, cache_vT):
    batch, n_pages = page_table.shape
    rows_pad = bias3.shape[1]
    return pl.pallas_call(
        _attn_sample_kernel,
        grid_spec=pltpu.PrefetchScalarGridSpec(
            num_scalar_prefetch=1,
            grid=(batch,),
            in_specs=[pl.BlockSpec((None, N_HEADS_A, LANES), lambda b, pt: (b, 0, 0)),
                      pl.BlockSpec((None, rows_pad, PAGE_SIZE), lambda b, pt: (b, 0, 0)),
                      pl.BlockSpec(k_new.shape, lambda b, pt: (0, 0)),
                      pl.BlockSpec(v_new.shape, lambda b, pt: (0, 0)),
                      pl.BlockSpec(memory_space=pl.ANY),
                      pl.BlockSpec(memory_space=pl.ANY)],
            out_specs=pl.BlockSpec((None, GROUP, LANES), lambda b, pt: (b, 0, 0)),
            scratch_shapes=[pltpu.VMEM((2, n_pages, N_KV_A, HEAD_DIM, PAGE_SIZE), F32),
                            pltpu.VMEM((2, n_pages, N_KV_A, HEAD_DIM, PAGE_SIZE), F32),
                            pltpu.VMEM((rows_pad, N_HEADS_A, PAGE_SIZE), F32),
                            pltpu.SemaphoreType.DMA((2,)), pltpu.SemaphoreType.DMA((2,))]),
        out_shape=jax.ShapeDtypeStruct((batch, GROUP, LANES), F32),
        compiler_params=pltpu.CompilerParams(dimension_semantics=("arbitrary",),
                                             vmem_limit_bytes=VMEM_LIMIT_BYTES),
        name="attn_sample",
    )(page_table, q8, bias3, k_new, v_new, cache_kT, cache_vT)


def _hgrn_sample_kernel(rq_ref, rf_ref, ri_ref, rg_ref, lb_ref, ng_ref, s_ref, o_ref, s_out_ref):
    eye = _eye(HEAD_DIM)
    lb = lb_ref[...]
    for j in range(s_ref.shape[0]):
        b = pl.program_id(0) * s_ref.shape[0] + j
        row = lambda ref: ref[pl.ds(b, 1), :]
        f = lb + (1.0 - lb) * jax.nn.sigmoid(row(rf_ref))
        q, k, v, g = row(rq_ref), 1.0 - f, row(ri_ref), row(rg_ref)
        gate = ng_ref[...] * (g * jax.nn.sigmoid(g))
        outs = []
        for h in range(N_HEADS_R):
            sl = slice(h * HEAD_DIM, (h + 1) * HEAD_DIM)
            s_new = _column(f[:, sl], eye) * s_ref[j, h] + _column(k[:, sl], eye) * v[:, sl]
            s_out_ref[j, h] = s_new
            o = jnp.sum(_column(q[:, sl], eye) * s_new, axis=0, keepdims=True)
            outs.append(o * lax.rsqrt(jnp.mean(o * o, axis=1, keepdims=True) + RMS_EPS))
        o_ref[pl.ds(b, 1), :] = jnp.concatenate(outs, axis=1) * gate


def _hgrn_sample(rq, rf, ri, rg, lb, norm_g, state):
    batch = rq.shape[0]
    per_step = HGRN_SAMPLE_GROUP if batch % HGRN_SAMPLE_GROUP == 0 else 1
    full = lambda a: pl.BlockSpec(a.shape, lambda b, _nd=a.ndim: (0,) * _nd)
    st = pl.BlockSpec((per_step, N_HEADS_R, HEAD_DIM, HEAD_DIM), lambda b: (b, 0, 0, 0))
    lb2, ng2 = lb.reshape(1, -1), norm_g.reshape(1, -1)
    return pl.pallas_call(
        _hgrn_sample_kernel,
        grid=(batch // per_step,),
        in_specs=[full(rq), full(rf), full(ri), full(rg), full(lb2), full(ng2), st],
        out_specs=[full(rq), st],
        out_shape=[jax.ShapeDtypeStruct(rq.shape, F32), jax.ShapeDtypeStruct(state.shape, F32)],
        compiler_params=pltpu.CompilerParams(dimension_semantics=("arbitrary",)),
        name="hgrn_sample",
    )(rq, rf, ri, rg, lb2, ng2, state)


FFN_CHUNK = 256
ROW_TILE = 512
Q_TILE = 256
KV_STEP = 256
HGRN_CHUNK = 128
HGRN_TILE = 1024
HGRN_SAMPLE_GROUP = 1


def kernel(x_prompt, x_sample, cache_k, cache_v, cache_kidx, state_hgrn, page_table, w_in, w_out, hgrn_lb,
           hgrn_norm_g, ffn1_wg, ffn1_wu, ffn1_wd, ffn2_wg, ffn2_wu, ffn2_wd, ln1_g, ln1_b, ln2_g, ln2_b,
           ln3_g, ln3_b):
    assert w_in.shape[0] == DEPTH == 1
    batch, seq, d_model = x_prompt.shape
    dec_batch, dec_seq, _ = x_sample.shape
    assert dec_seq == 1
    n_pages = page_table.shape[1]
    l = 0

    lb = jnp.cumsum(jax.nn.softmax(hgrn_lb.astype(F32), axis=0), axis=0)[l]
    w1 = _ffn_weights(ffn1_wg[l], ffn1_wu[l], ffn1_wd[l])
    w2 = _ffn_weights(ffn2_wg[l], ffn2_wu[l], ffn2_wd[l])
    w_cat = _proj_weight(w_in[l])
    wo = w_out[l].astype(BF16)
    wo_a, wo_r = wo[:A_Q][_q_perm()], wo[A_Q:]

    xp = x_prompt.reshape(batch * seq, d_model)
    hp = _ffn(xp, w1, ln1_g[l], ln1_b[l], ROW_TILE)
    (q, k_t, v_t, kb_t, vb, qi, ki_t, kk_t, wi, rq, rf, ri, rg) = _proj(hp, w_cat, ROW_TILE, seq=seq)
    k = k_t.reshape(batch, N_KV_A, HEAD_DIM, seq).transpose(0, 3, 1, 2)
    v = v_t.reshape(batch, N_KV_A, HEAD_DIM, seq).transpose(0, 3, 1, 2)
    ki = ki_t.transpose(0, 2, 1)
    oa = _attn_prompt(q, qi, wi, kk_t, kb_t, vb, batch, Q_TILE)
    orr, sp = _hgrn_prompt(rq, rf, ri, rg, lb, hgrn_norm_g[l], batch, HGRN_CHUNK, HGRN_TILE)
    yp = _ffn(hp, w2, ln3_g[l], ln3_b[l], ROW_TILE, mix=(oa, orr, wo_a, wo_r, ln2_g[l], ln2_b[l]))

    xs = x_sample.reshape(dec_batch, d_model)
    hs = _ffn(xs, w1, ln1_g[l], ln1_b[l], dec_batch)
    (qs, ks, vs, _, _, qis, kis, _, wis, rqs, rfs, ris, rgs) = _proj(hs, w_cat, dec_batch)
    rows_pad = -(-(n_pages + 1) // SUBLANES) * SUBLANES
    cache_kiT = jnp.transpose(cache_kidx[l], (0, 2, 1))
    cache_kT = jnp.transpose(cache_k[l], (0, 2, 3, 1))
    cache_vT = jnp.transpose(cache_v[l], (0, 2, 3, 1))
    scores = _score_sample(page_table, qis.reshape(dec_batch, N_HEADS_IDX, HEAD_DIM), wis, kis, cache_kiT,
                           rows_pad)
    n_sel = min(TOPK_MAX, (n_pages * PAGE_SIZE + 1) // 4)
    bias = _select_sample(scores.reshape(dec_batch, rows_pad * PAGE_SIZE), n_sel)
    q_heads = qs.reshape(dec_batch, GROUP, 2, HEAD_DIM).transpose(0, 2, 1, 3).reshape(
        dec_batch, N_HEADS_A, HEAD_DIM)
    zeros = jnp.zeros_like(q_heads)
    q8 = jnp.concatenate([jnp.concatenate([q_heads[:, :GROUP], zeros[:, :GROUP]], -1),
                          jnp.concatenate([zeros[:, GROUP:], q_heads[:, GROUP:]], -1)], 1)
    oas = _attn_sample(page_table, q8, bias.reshape(dec_batch, rows_pad, PAGE_SIZE), ks, vs,
                       cache_kT, cache_vT).reshape(dec_batch, A_Q).astype(BF16)
    ors, ss = _hgrn_sample(rqs, rfs, ris, rgs, lb, hgrn_norm_g[l], state_hgrn[l])
    ys = _ffn(hs, w2, ln3_g[l], ln3_b[l], dec_batch,
              mix=(oas, ors.astype(BF16), wo_a, wo_r, ln2_g[l], ln2_b[l]))

    return (yp.reshape(batch, seq, d_model), ys.reshape(dec_batch, 1, d_model),
            k[None], v[None], ki[None], sp[None],
            ks.reshape(1, dec_batch, 1, N_KV_A, HEAD_DIM), vs.reshape(1, dec_batch, 1, N_KV_A, HEAD_DIM),
            kis.reshape(1, dec_batch, 1, HEAD_DIM), ss[None])
```

```python
import functools

import numpy as np
import jax
import jax.numpy as jnp
from jax import lax
from jax.experimental import pallas as pl
from jax.experimental.pallas import tpu as pltpu

F32 = jnp.float32
BF16 = jnp.bfloat16
I32 = jnp.int32

DEPTH = 1
N_HEADS_A = 8
N_KV_A = 2
HEAD_DIM = 64
N_HEADS_IDX = 8
N_HEADS_R = 8
TOPK_MAX = 256
PAGE_SIZE = 128
LN_EPS = 1e-5
RMS_EPS = 1e-6
DN_ALPHA = (2 * DEPTH) ** 0.25

LANES = 128
SUBLANES = 8
VMEM_LIMIT_BYTES = 56 * 1024 * 1024

NEG_INF = float("-inf")
INT_MIN = -(2 ** 31)


def _dot(a, b):
    return jnp.dot(a, b, preferred_element_type=F32)


def _dot_nt(a, b):
    return lax.dot_general(a, b, (((1,), (1,)), ((), ())), preferred_element_type=F32)


def _dot_tn(a, b):
    return lax.dot_general(a, b, (((0,), (0,)), ((), ())), preferred_element_type=F32)


def _split3(x):
    hi = x.astype(BF16)
    r1 = x - hi.astype(F32)
    mid = r1.astype(BF16)
    lo = (r1 - mid.astype(F32)).astype(BF16)
    return hi, mid, lo


def _dot_exact_lhs(m_bf, x):
    hi, mid, lo = _split3(x)
    return _dot(m_bf, hi) + _dot(m_bf, mid) + _dot(m_bf, lo)


def _dot_exact_rhs(x, m_bf):
    hi, mid, lo = _split3(x)
    return _dot(hi, m_bf) + _dot(mid, m_bf) + _dot(lo, m_bf)


def _layer_norm(x, g, b):
    mu = jnp.mean(x, -1, keepdims=True)
    xc = x - mu
    var = jnp.mean(xc * xc, -1, keepdims=True)
    return xc * lax.rsqrt(var + LN_EPS) * g + b


def _ffn_kernel(*refs, fc, with_mix):
    if with_mix:
        (h_ref, oa_ref, or_ref, woa_ref, wor_ref, g2_ref, b2_ref,
         wg_ref, wu_ref, wd_ref, g_ref, b_ref, o_ref, acc_ref, x_ref) = refs
        m = _dot(oa_ref[...], woa_ref[...]) + _dot(or_ref[...], wor_ref[...])
        x_ref[...] = _layer_norm(DN_ALPHA * h_ref[...] + m, g2_ref[...], b2_ref[...])
    else:
        x_ref, wg_ref, wu_ref, wd_ref, g_ref, b_ref, o_ref, acc_ref = refs
    xb = x_ref[...].astype(BF16)
    acc_ref[...] = jnp.zeros_like(acc_ref)
    for c0 in range(0, wg_ref.shape[1], fc):
        gg = _dot(xb, wg_ref[:, c0:c0 + fc])
        uu = _dot(xb, wu_ref[:, c0:c0 + fc])
        hh = (gg * jax.nn.sigmoid(gg) * uu).astype(BF16)
        acc_ref[...] += _dot(hh, wd_ref[c0:c0 + fc, :])
    o_ref[...] = _layer_norm(DN_ALPHA * x_ref[...] + 0.5 * acc_ref[...], g_ref[...], b_ref[...])


def _const_spec(shape):
    nd = len(shape)
    return pl.BlockSpec(shape, lambda i, _nd=nd: (0,) * _nd, pipeline_mode=pl.Buffered(1))


def _ffn_weights(wg, wu, wd):
    return wg.astype(BF16), wu.astype(BF16), wd.astype(BF16)


def _ffn(x, w3, g, b, tm, mix=None):
    wg3, wu3, wd3 = w3
    n_rows, d = x.shape
    assert n_rows % tm == 0 and wg3.shape[1] % FFN_CHUNK == 0
    row = lambda w: pl.BlockSpec((tm, w), lambda i: (i, 0))
    vec = lambda a: a.reshape(1, -1).astype(F32)
    args, specs = [x], [row(d)]
    scratch = [pltpu.VMEM((tm, d), F32)]
    if mix is not None:
        oa, orr, woa, wor, g2, b2 = mix
        args += [oa, orr, woa, wor, vec(g2), vec(b2)]
        specs += [row(oa.shape[1]), row(orr.shape[1]), _const_spec(woa.shape), _const_spec(wor.shape),
                  _const_spec((1, d)), _const_spec((1, d))]
        scratch.append(pltpu.VMEM((tm, d), F32))
    args += [wg3, wu3, wd3, vec(g), vec(b)]
    specs += [_const_spec(wg3.shape), _const_spec(wu3.shape), _const_spec(wd3.shape),
              _const_spec((1, d)), _const_spec((1, d))]
    return pl.pallas_call(
        functools.partial(_ffn_kernel, fc=FFN_CHUNK, with_mix=mix is not None),
        grid=(n_rows // tm,),
        in_specs=specs,
        out_specs=row(d),
        out_shape=jax.ShapeDtypeStruct((n_rows, d), F32),
        scratch_shapes=scratch,
        compiler_params=pltpu.CompilerParams(dimension_semantics=("arbitrary",),
                                             vmem_limit_bytes=VMEM_LIMIT_BYTES),
        name="ffn_mix" if mix is not None else "ffn",
    )(*args)


A_Q = N_HEADS_A * HEAD_DIM
A_KV = N_KV_A * HEAD_DIM
I_Q = N_HEADS_IDX * HEAD_DIM
R_W = N_HEADS_R * HEAD_DIM
_SPLIT = (A_Q, A_KV, A_KV, I_Q, HEAD_DIM, N_HEADS_IDX, R_W, R_W, R_W, R_W)
_OFF = tuple(int(v) for v in np.cumsum((0,) + _SPLIT))
GROUP = N_HEADS_A // N_KV_A


def _q_perm():
    idx = []
    for j in range(GROUP):
        idx += list(range(j * HEAD_DIM, (j + 1) * HEAD_DIM))
        idx += list(range((GROUP + j) * HEAD_DIM, (GROUP + j + 1) * HEAD_DIM))
    return np.asarray(idx, np.int32)


def _proj_weight(w_in):
    seg = lambda i: w_in[:, _OFF[i]:_OFF[i + 1]]
    q = seg(0)[:, _q_perm()]
    ki = seg(4)
    wi = jnp.pad(seg(5), ((0, 0), (0, LANES - N_HEADS_IDX)))
    cols = [q, seg(1), seg(2), seg(3), ki, ki, wi, seg(6), seg(7), seg(8), seg(9)]
    return jnp.concatenate(cols, axis=1).astype(BF16)


def _proj_kernel(h_ref, w_ref, q_ref, k_ref, v_ref, kb_ref, vb_ref, qi_ref, ki_ref, kk_ref, wi_ref,
                 rq_ref, rf_ref, ri_ref, rg_ref, *, keys_on_lanes):
    tr = (lambda a: a.T) if keys_on_lanes else (lambda a: a)
    hb = h_ref[...].astype(BF16)
    col = [0]

    def take(width):
        out = _dot(hb, w_ref[:, col[0]:col[0] + width])
        col[0] += width
        return out

    scale = HEAD_DIM ** -0.5
    q_ref[...] = (take(A_Q) * scale).astype(BF16)
    k = tr(take(A_KV))
    k_ref[...] = k
    kb_ref[...] = k.astype(BF16)
    v = take(A_KV)
    v_ref[...] = tr(v)
    vb_ref[...] = v.astype(BF16)
    qi_ref[...] = (take(I_Q) * scale).astype(BF16)
    kk = tr(take(LANES))
    ki_ref[...] = kk[:HEAD_DIM, :] if keys_on_lanes else kk[:, :HEAD_DIM]
    kk_ref[...] = kk.astype(BF16)
    wi_ref[...] = take(LANES) * (N_HEADS_IDX ** -0.5)
    rq_ref[...] = take(R_W)
    rf_ref[...] = take(R_W)
    ri_ref[...] = take(R_W)
    rg_ref[...] = take(R_W)


def _proj(h, w_cat, tm, seq=None):
    n_rows, d = h.shape
    assert n_rows % tm == 0
    row = lambda w: pl.BlockSpec((tm, w), lambda i: (i, 0))
    outs = [(A_Q, BF16), (A_KV, F32), (A_KV, F32), (A_KV, BF16), (A_KV, BF16), (I_Q, BF16),
            (HEAD_DIM, F32), (LANES, BF16), (LANES, F32), (R_W, F32), (R_W, F32), (R_W, F32), (R_W, F32)]
    out_specs = [row(w) for w, _ in outs]
    out_shape = [jax.ShapeDtypeStruct((n_rows, w), dt) for w, dt in outs]
    if seq is not None:
        assert seq % tm == 0 and n_rows % seq == 0
        per_seq = seq // tm
        for i in (1, 2, 3, 6, 7):
            w, dt = outs[i]
            out_specs[i] = pl.BlockSpec((None, w, tm), lambda i, _p=per_seq: (i // _p, 0, i % _p))
            out_shape[i] = jax.ShapeDtypeStruct((n_rows // seq, w, seq), dt)
    return pl.pallas_call(
        functools.partial(_proj_kernel, keys_on_lanes=seq is not None),
        grid=(n_rows // tm,),
        in_specs=[row(d), _const_spec(w_cat.shape)],
        out_specs=out_specs,
        out_shape=out_shape,
        compiler_params=pltpu.CompilerParams(dimension_semantics=("arbitrary",),
                                             vmem_limit_bytes=VMEM_LIMIT_BYTES),
        name="in_proj",
    )(h, w_cat)


def _ordered_float(code):
    key = code ^ jnp.int32(INT_MIN)
    bits = key ^ ((key >> 31) & jnp.int32(0x7FFFFFFF))
    f = pltpu.bitcast(bits, F32)
    return jnp.where(f != f, jnp.where(key >= 0, jnp.inf, NEG_INF), f)


def _count(mask):
    return jnp.sum(jnp.where(mask, 1.0, 0.0), axis=1, keepdims=True)


def _kth_largest(score_ref, n_sel):
    rows = score_ref.shape[0]

    def body(i, code):
        cand = code | jnp.left_shift(jnp.int32(1), 31 - i)
        cnt = _count(score_ref[...] >= _ordered_float(cand))
        return jnp.where(cnt >= n_sel, cand, code)

    return _ordered_float(lax.fori_loop(0, 32, body, jnp.zeros((rows, 1), I32)))


def _kth_largest_rows_on_lanes(score_t_ref, n_sel):
    n_keys, rows = score_t_ref.shape
    lanes_of_sums = 8
    assert n_keys % (SUBLANES * lanes_of_sums) == 0

    def body(i, code):
        cand = code | jnp.left_shift(jnp.int32(1), 31 - i)
        level = jnp.broadcast_to(_ordered_float(cand), (SUBLANES, rows))
        sums = [jnp.zeros((SUBLANES, rows), F32) for _ in range(lanes_of_sums)]
        for t in range(n_keys // SUBLANES):
            acc = sums[t % lanes_of_sums]
            tile = score_t_ref[t * SUBLANES:(t + 1) * SUBLANES, :]
            sums[t % lanes_of_sums] = jnp.where(tile >= level, acc + 1.0, acc)
        cnt = jnp.sum(functools.reduce(jnp.add, sums), axis=0, keepdims=True)
        return jnp.where(cnt >= n_sel, cand, code)

    return _ordered_float(lax.fori_loop(0, 32, body, jnp.zeros((1, rows), I32)))


def _tie_cut(score_ref, thr, idx, n_sel, n_bits):
    rows, width = score_ref.shape
    need = n_sel - _count(score_ref[...] > thr)

    def body(i, j):
        cand = j | jnp.left_shift(jnp.int32(1), n_bits - 1 - i)
        cnt = _count((score_ref[...] == thr) & (idx < cand))
        return jnp.where(cnt <= need, cand, j)

    return lax.fori_loop(0, n_bits, body, jnp.zeros((rows, 1), I32))


def _select_mask(score_ref, idx, n_sel, n_bits, cut_ref, thr=None):
    if thr is None:
        thr = _kth_largest(score_ref, n_sel)
    score = score_ref[...]
    eq = score == thr
    excess = ((_count(score > thr) + _count(eq)) > n_sel) & (thr != NEG_INF)
    cut_ref[...] = jnp.full(cut_ref.shape, 1 << n_bits, I32)

    @pl.when(jnp.max(jnp.where(excess, 1.0, 0.0)) > 0.0)
    def _():
        cut_ref[...] = _tie_cut(score_ref, thr, idx, n_sel, n_bits + 1)

    return (score > thr) | (eq & (idx < cut_ref[...]))


def _attn_prompt_kernel(q_ref, qi_ref, wi_ref, kk_ref, k_ref, v_ref, o_ref, score_ref, score_t_ref, cut_ref, *,
                        n_sel, row0):
    tq = q_ref.shape[0]
    s_len = v_ref.shape[0]
    n_bits = (s_len - 1).bit_length()
    q_pos = row0 + pl.program_id(1) * tq + lax.broadcasted_iota(I32, (tq, 1), 0)
    key_pos = lax.broadcasted_iota(I32, (tq, s_len), 1)
    causal = key_pos <= q_pos
    low = lax.broadcasted_iota(I32, (1, LANES), 1) < HEAD_DIM
    zero = jnp.zeros((), BF16)

    kk = kk_ref[...]
    score = jnp.zeros((tq, s_len), F32)
    for j in range(I_Q // LANES):
        blk = qi_ref[:, j * LANES:(j + 1) * LANES]
        for half in range(2):
            h = 2 * j + half
            qm = jnp.where(low if half == 0 else ~low, blk, zero)
            score = score + wi_ref[:, h:h + 1] * jnp.maximum(_dot(qm, kk), 0.0)
    score = jnp.where(causal, score, NEG_INF)
    score_ref[...] = score
    score_t_ref[...] = score.T
    thr = _column(_kth_largest_rows_on_lanes(score_t_ref, n_sel), _eye(tq))
    sel = _select_mask(score_ref, key_pos, n_sel, n_bits, cut_ref, thr=thr) & causal
    bias = jnp.where(sel, 0.0, NEG_INF)

    kb = k_ref[...]
    vb = v_ref[...]
    for j in range(GROUP):
        blk = q_ref[:, j * LANES:(j + 1) * LANES]
        outs = []
        for half in range(2):
            qm = jnp.where(low if half == 0 else ~low, blk, zero)
            s = _dot(qm, kb) + bias
            m = jnp.max(s, axis=1, keepdims=True)
            p = jnp.exp(s - m)
            l = jnp.sum(p, axis=1, keepdims=True)
            outs.append(_dot(p.astype(BF16), vb) / l)
        o_ref[:, j * LANES:(j + 1) * LANES] = jnp.where(low, outs[0], outs[1]).astype(o_ref.dtype)


def _attn_prompt(q, qi, wi, kk_t, kb_t, vb, batch, tq):
    n_rows = q.shape[0]
    s_len = n_rows // batch
    assert s_len & (s_len - 1) == 0 and s_len % tq == 0
    n_sel = min(TOPK_MAX, s_len // 4)
    kv_step = min(KV_STEP, s_len)
    assert kv_step >= n_sel and kv_step % tq == 0 and s_len % kv_step == 0
    vb3 = vb.reshape(batch, s_len, A_KV)
    pieces = []
    for n_kv in range(kv_step, s_len + 1, kv_step):
        first = (n_kv - kv_step) // tq
        tile = lambda w, _f=first: pl.BlockSpec((tq, w), lambda b, i: (b * (s_len // tq) + _f + i, 0))
        keys_t = lambda w: pl.BlockSpec((None, w, n_kv), lambda b, i: (b, 0, 0))
        pieces.append(pl.pallas_call(
            functools.partial(_attn_prompt_kernel, n_sel=n_sel, row0=n_kv - kv_step),
            grid=(batch, kv_step // tq),
            in_specs=[tile(A_Q), tile(I_Q), tile(LANES), keys_t(LANES), keys_t(A_KV),
                      pl.BlockSpec((None, n_kv, A_KV), lambda b, i: (b, 0, 0))],
            out_specs=pl.BlockSpec((None, tq, A_Q), lambda b, i: (b, i, 0)),
            out_shape=jax.ShapeDtypeStruct((batch, kv_step, A_Q), BF16),
            scratch_shapes=[pltpu.VMEM((tq, n_kv), F32), pltpu.VMEM((n_kv, tq), F32), pltpu.VMEM((tq, 1), I32)],
            compiler_params=pltpu.CompilerParams(dimension_semantics=("arbitrary", "arbitrary"),
                                                 vmem_limit_bytes=VMEM_LIMIT_BYTES),
            name=f"attn_prompt_{n_kv}",
        )(q, qi, wi, kk_t, kb_t, vb3))
    return jnp.concatenate(pieces, axis=1).reshape(n_rows, A_Q)


def _hgrn_consts(c):
    t = np.arange(c)
    tri = (t[None, :] <= t[:, None]).astype(np.float32)
    masks = []
    hs = c // 2
    while hs >= 1:
        blk = t // (2 * hs)
        right = (t % (2 * hs)) >= hs
        masks.append(((blk[:, None] == blk[None, :]) & right[:, None] & ~right[None, :]).astype(np.float32))
        hs //= 2
    return jnp.asarray(tri, BF16), jnp.asarray(np.stack(masks, 0), F32)


def _head_block_ones():
    d = np.arange(LANES) // HEAD_DIM
    return (d[:, None] == d[None, :]).astype(np.float32)


def _level_reference(b_ref, hs):
    c = b_ref.shape[0]
    row = lambda r, n: jnp.broadcast_to(b_ref[r:r + 1, :], (n, LANES))
    if 2 * hs >= SUBLANES:
        return jnp.concatenate([row(s + hs - 1, 2 * hs) for s in range(0, c, 2 * hs)], axis=0)
    sub = lax.broadcasted_iota(I32, (SUBLANES, 1), 0)
    tiles = []
    for s in range(0, c, SUBLANES):
        e = row(s + hs - 1, SUBLANES)
        for off in range(2 * hs, SUBLANES, 2 * hs):
            e = jnp.where(sub >= off, row(s + off + hs - 1, SUBLANES), e)
        tiles.append(e)
    return jnp.concatenate(tiles, axis=0)


def _hgrn_prompt_kernel(rq_ref, rf_ref, ri_ref, rg_ref, lb_ref, ng_ref, tri_ref, msk_ref, bo_ref,
                        o_ref, s_out_ref, st_ref, b_ref):
    c = tri_ref.shape[0]
    n_sub = rq_ref.shape[0] // c
    n_levels = msk_ref.shape[0]
    step = pl.program_id(1)

    @pl.when(step == 0)
    def _():
        st_ref[...] = jnp.zeros_like(st_ref)

    bo = bo_ref[...]
    bo_bf = bo.astype(BF16)
    same_head = bo > 0.5
    low = lax.broadcasted_iota(I32, (1, LANES), 1) < HEAD_DIM
    for p in range(rq_ref.shape[1] // LANES):
        sl = slice(p * LANES, (p + 1) * LANES)
        lb = lb_ref[:, sl]
        st = st_ref[p]
        for ci in range(n_sub):
            rows = slice(ci * c, (ci + 1) * c)
            f = lb + (1.0 - lb) * jax.nn.sigmoid(rf_ref[rows, sl])
            q = rq_ref[rows, sl]
            k = 1.0 - f
            v = ri_ref[rows, sl]
            vb = v.astype(BF16)

            b = _dot_exact_lhs(tri_ref[...], jnp.log(f))
            b_ref[p * n_sub + ci] = b
            o = _dot_nt((q * jnp.exp(b)).astype(BF16), st.astype(BF16))
            o = o + _dot((q * k).astype(BF16), bo_bf) * v

            a0 = jnp.zeros((c, c), F32)
            a1 = jnp.zeros((c, c), F32)
            for lv in range(n_levels):
                diff = b - _level_reference(b_ref.at[p * n_sub + ci], c >> (lv + 1))
                qdl = q * jnp.exp(jnp.minimum(diff, 0.0))
                kdl = (k * jnp.exp(jnp.minimum(-diff, 0.0))).astype(BF16)
                msk = msk_ref[lv] > 0.5
                a0 = a0 + jnp.where(msk, _dot_nt(jnp.where(low, qdl, 0.0).astype(BF16), kdl), 0.0)
                a1 = a1 + jnp.where(msk, _dot_nt(jnp.where(low, 0.0, qdl).astype(BF16), kdl), 0.0)
            o = o + jnp.where(low, _dot(a0.astype(BF16), vb), _dot(a1.astype(BF16), vb))

            b_last = b[c - 1:c, :]
            kdec = (k * jnp.exp(jnp.minimum(b_last - b, 0.0))).astype(BF16)
            st = st * jnp.exp(b_last) + jnp.where(same_head, _dot_tn(vb, kdec), 0.0)

            ms = _dot_exact_rhs(o * o, bo_bf) * (1.0 / HEAD_DIM)
            g = rg_ref[rows, sl]
            o_ref[rows, sl] = (o * lax.rsqrt(ms + RMS_EPS) * ng_ref[:, sl]
                               * (g * jax.nn.sigmoid(g))).astype(o_ref.dtype)
        st_ref[p] = st

        @pl.when(step == pl.num_programs(1) - 1)
        def _(p=p, st=st):
            s_out_ref[p] = st.T


def _hgrn_prompt(rq, rf, ri, rg, lb, norm_g, batch, chunk, rows_per_step):
    n_rows, width = rq.shape
    t_len = n_rows // batch
    assert chunk & (chunk - 1) == 0 and chunk >= SUBLANES
    assert rows_per_step % chunk == 0 and t_len % rows_per_step == 0
    nc = t_len // rows_per_step
    n_pairs = width // LANES
    tri, msk = _hgrn_consts(chunk)
    bo = jnp.asarray(_head_block_ones(), F32)
    tile = pl.BlockSpec((rows_per_step, width), lambda b, i: (b * nc + i, 0))
    const = lambda a: pl.BlockSpec(a.shape, lambda b, i, _nd=a.ndim: (0,) * _nd)
    lb2, ng2 = lb.reshape(1, -1), norm_g.reshape(1, -1)
    o, s_pairs = pl.pallas_call(
        _hgrn_prompt_kernel,
        grid=(batch, nc),
        in_specs=[tile, tile, tile, tile, const(lb2), const(ng2), const(tri), const(msk), const(bo)],
        out_specs=[tile, pl.BlockSpec((None, n_pairs, LANES, LANES), lambda b, i: (b, 0, 0, 0))],
        out_shape=[jax.ShapeDtypeStruct((n_rows, width), BF16),
                   jax.ShapeDtypeStruct((batch, n_pairs, LANES, LANES), F32)],
        scratch_shapes=[pltpu.VMEM((n_pairs, LANES, LANES), F32),
                        pltpu.VMEM((n_pairs * (rows_per_step // chunk), chunk, LANES), F32)],
        compiler_params=pltpu.CompilerParams(dimension_semantics=("arbitrary", "arbitrary"),
                                             vmem_limit_bytes=VMEM_LIMIT_BYTES),
        name="hgrn_prompt",
    )(rq, rf, ri, rg, lb2, ng2, tri, msk, bo)
    s0 = s_pairs[:, :, :HEAD_DIM, :HEAD_DIM]
    s1 = s_pairs[:, :, HEAD_DIM:, HEAD_DIM:]
    return o, jnp.stack([s0, s1], axis=2).reshape(batch, N_HEADS_R, HEAD_DIM, HEAD_DIM)


def _column(row, eye):
    return jnp.sum(jnp.where(eye, row, 0.0), axis=1, keepdims=True)


def _eye(n):
    return lax.broadcasted_iota(I32, (n, n), 0) == lax.broadcasted_iota(I32, (n, n), 1)


PAGE_UNROLL = 64


def _fetch_pages(pt_ref, b, n_pages, src_ref, dst_ref, sem):
    def start(p, carry):
        pltpu.make_async_copy(src_ref.at[pt_ref[b, p]], dst_ref.at[p], sem).start()
        return carry

    lax.fori_loop(0, n_pages, start, 0, unroll=PAGE_UNROLL)


def _wait_pages(n_pages, src_ref, dst_ref, sem):
    def wait(p, carry):
        pltpu.make_async_copy(src_ref.at[0], dst_ref.at[p], sem).wait()
        return carry

    lax.fori_loop(0, n_pages, wait, 0, unroll=PAGE_UNROLL)


def _prefetch_sequence(pt_ref, n_pages, streams):
    b = pl.program_id(0)
    slot = lax.rem(b, 2)

    @pl.when(b == 0)
    def _():
        for src, buf, sem in streams:
            _fetch_pages(pt_ref, 0, n_pages, src, buf.at[0], sem.at[0])

    @pl.when(b + 1 < pl.num_programs(0))
    def _():
        for src, buf, sem in streams:
            _fetch_pages(pt_ref, b + 1, n_pages, src, buf.at[1 - slot], sem.at[1 - slot])

    return slot


def _score_sample_kernel(pt_ref, qi_ref, wi_ref, kin_ref, cache_ref, o_ref, kbuf2, sem):
    b = pl.program_id(0)
    n_pages = kbuf2.shape[1]
    slot = _prefetch_sequence(pt_ref, n_pages, [(cache_ref, kbuf2, sem)])
    kbuf = kbuf2.at[slot]
    qi = qi_ref[...]
    wcol = _column(wi_ref[pl.ds(b, 1), :][:, :N_HEADS_IDX], _eye(N_HEADS_IDX))
    lane0 = lax.broadcasted_iota(I32, (1, PAGE_SIZE), 1) == 0
    ki_new = kin_ref[pl.ds(b, 1), :].astype(BF16).astype(F32)
    lg_new = jnp.sum(qi.astype(F32) * ki_new, axis=1, keepdims=True)
    sc_new = jnp.sum(wcol * jnp.maximum(lg_new, 0.0), axis=0, keepdims=True)
    o_ref[...] = jnp.full(o_ref.shape, NEG_INF, F32)
    o_ref[pl.ds(n_pages, 1), :] = jnp.where(lane0, sc_new, NEG_INF)
    _wait_pages(n_pages, cache_ref, kbuf, sem.at[slot])

    def page(p, carry):
        lg = _dot(qi, kbuf[p].astype(BF16))
        o_ref[pl.ds(p, 1), :] = jnp.sum(wcol * jnp.maximum(lg, 0.0), axis=0, keepdims=True)
        return carry

    lax.fori_loop(0, n_pages, page, 0, unroll=PAGE_UNROLL)


def _score_sample(page_table, qi3, wi, ki_new, cache_kiT, rows_pad):
    batch, n_pages = page_table.shape
    return pl.pallas_call(
        _score_sample_kernel,
        grid_spec=pltpu.PrefetchScalarGridSpec(
            num_scalar_prefetch=1,
            grid=(batch,),
            in_specs=[pl.BlockSpec((None, N_HEADS_IDX, HEAD_DIM), lambda b, pt: (b, 0, 0)),
                      pl.BlockSpec(wi.shape, lambda b, pt: (0, 0)),
                      pl.BlockSpec(ki_new.shape, lambda b, pt: (0, 0)),
                      pl.BlockSpec(memory_space=pl.ANY)],
            out_specs=pl.BlockSpec((None, rows_pad, PAGE_SIZE), lambda b, pt: (b, 0, 0)),
            scratch_shapes=[pltpu.VMEM((2, n_pages, HEAD_DIM, PAGE_SIZE), F32), pltpu.SemaphoreType.DMA((2,))]),
        out_shape=jax.ShapeDtypeStruct((batch, rows_pad, PAGE_SIZE), F32),
        compiler_params=pltpu.CompilerParams(dimension_semantics=("arbitrary",),
                                             vmem_limit_bytes=VMEM_LIMIT_BYTES),
        name="score_sample",
    )(page_table, qi3, wi, ki_new, cache_kiT)


def _select_sample_kernel(s_ref, o_ref, cut_ref, *, n_sel):
    rows, width = s_ref.shape
    n_bits = (width - 1).bit_length()
    idx = lax.broadcasted_iota(I32, (rows, width), 1)
    sel = _select_mask(s_ref, idx, n_sel, n_bits, cut_ref)
    o_ref[...] = jnp.where(sel, 0.0, NEG_INF)


def _select_sample(scores, n_sel):
    rows, width = scores.shape
    return pl.pallas_call(
        functools.partial(_select_sample_kernel, n_sel=n_sel),
        out_shape=jax.ShapeDtypeStruct((rows, width), F32),
        scratch_shapes=[pltpu.VMEM((rows, 1), I32)],
        compiler_params=pltpu.CompilerParams(vmem_limit_bytes=VMEM_LIMIT_BYTES),
        name="select_sample",
    )(scores)


def _attn_sample_kernel(pt_ref, q_ref, bias_ref, kn_ref, vn_ref, ck_ref, cv_ref, o_ref,
                        kbuf2, vbuf2, s_ref, ksem, vsem):
    b = pl.program_id(0)
    n_pages = kbuf2.shape[1]
    slot = _prefetch_sequence(pt_ref, n_pages, [(ck_ref, kbuf2, ksem), (cv_ref, vbuf2, vsem)])
    kbuf, vbuf = kbuf2.at[slot], vbuf2.at[slot]
    q = q_ref[...]
    lane_low = lax.broadcasted_iota(I32, (1, LANES), 1) < HEAD_DIM
    lane0 = lax.broadcasted_iota(I32, (1, PAGE_SIZE), 1) == 0
    kn = kn_ref[pl.ds(b, 1), :]
    vn = vn_ref[pl.ds(b, 1), :]
    s_new = jnp.sum(q.astype(F32) * kn.astype(BF16).astype(F32), axis=1, keepdims=True)
    s_ref[n_pages] = jnp.where(lane0, s_new, NEG_INF) + bias_ref[pl.ds(n_pages, 1), :]
    _wait_pages(n_pages, ck_ref, kbuf, ksem.at[slot])

    def qk(p, m):
        kt = kbuf[p].reshape(A_KV, PAGE_SIZE).astype(BF16)
        s = _dot(q, kt) + bias_ref[pl.ds(p, 1), :]
        s_ref[p] = s
        return jnp.maximum(m, s)

    m = lax.fori_loop(0, n_pages, qk, s_ref[n_pages], unroll=PAGE_UNROLL)
    m = jnp.max(m, axis=1, keepdims=True)
    _wait_pages(n_pages, cv_ref, vbuf, vsem.at[slot])

    def pv(p, carry):
        acc, l = carry
        e = jnp.exp(s_ref[p] - m)
        vt = vbuf[p].reshape(A_KV, PAGE_SIZE).astype(BF16)
        return acc + _dot_nt(e.astype(BF16), vt), l + e

    e_new = jnp.exp(s_ref[n_pages] - m)
    acc0 = jnp.sum(e_new, axis=1, keepdims=True).astype(BF16).astype(F32) * vn.astype(BF16).astype(F32)
    acc, l = lax.fori_loop(0, n_pages, pv, (acc0, e_new), unroll=PAGE_UNROLL)
    out = acc / jnp.sum(l, axis=1, keepdims=True)
    o_ref[...] = jnp.where(lane_low, out[:GROUP], out[GROUP:]).astype(o_ref.dtype)


def _attn_sample(page_table, q8, bias3, k_new, v_new, cache_kT, cache_vT):
    batch, n_pages = page_table.shape
    rows_pad = bias3.shape[1]
    return pl.pallas_call(
        _attn_sample_kernel,
        grid_spec=pltpu.PrefetchScalarGridSpec(
            num_scalar_prefetch=1,
            grid=(batch,),
            in_specs=[pl.BlockSpec((None, N_HEADS_A, LANES), lambda b, pt: (b, 0, 0)),
                      pl.BlockSpec((None, rows_pad, PAGE_SIZE), lambda b, pt: (b, 0, 0)),
                      pl.BlockSpec(k_new.shape, lambda b, pt: (0, 0)),
                      pl.BlockSpec(v_new.shape, lambda b, pt: (0, 0)),
                      pl.BlockSpec(memory_space=pl.ANY),
                      pl.BlockSpec(memory_space=pl.ANY)],
            out_specs=pl.BlockSpec((None, GROUP, LANES), lambda b, pt: (b, 0, 0)),
            scratch_shapes=[pltpu.VMEM((2, n_pages, N_KV_A, HEAD_DIM, PAGE_SIZE), F32),
                            pltpu.VMEM((2, n_pages, N_KV_A, HEAD_DIM, PAGE_SIZE), F32),
                            pltpu.VMEM((rows_pad, N_HEADS_A, PAGE_SIZE), F32),
                            pltpu.SemaphoreType.DMA((2,)), pltpu.SemaphoreType.DMA((2,))]),
        out_shape=jax.ShapeDtypeStruct((batch, GROUP, LANES), F32),
        compiler_params=pltpu.CompilerParams(dimension_semantics=("arbitrary",),
                                             vmem_limit_bytes=VMEM_LIMIT_BYTES),
        name="attn_sample",
    )(page_table, q8, bias3, k_new, v_new, cache_kT, cache_vT)


def _hgrn_sample_kernel(rq_ref, rf_ref, ri_ref, rg_ref, lb_ref, ng_ref, s_ref, o_ref, s_out_ref):
    b = pl.program_id(0)
    eye = _eye(HEAD_DIM)
    row = lambda ref: ref[pl.ds(b, 1), :]
    lb = lb_ref[...]
    f = lb + (1.0 - lb) * jax.nn.sigmoid(row(rf_ref))
    q, k, v, g = row(rq_ref), 1.0 - f, row(ri_ref), row(rg_ref)
    gate = ng_ref[...] * (g * jax.nn.sigmoid(g))
    outs = []
    for h in range(N_HEADS_R):
        sl = slice(h * HEAD_DIM, (h + 1) * HEAD_DIM)
        s_new = _column(f[:, sl], eye) * s_ref[h] + _column(k[:, sl], eye) * v[:, sl]
        s_out_ref[h] = s_new
        o = jnp.sum(_column(q[:, sl], eye) * s_new, axis=0, keepdims=True)
        outs.append(o * lax.rsqrt(jnp.mean(o * o, axis=1, keepdims=True) + RMS_EPS))
    o_ref[pl.ds(b, 1), :] = jnp.concatenate(outs, axis=1) * gate


def _hgrn_sample(rq, rf, ri, rg, lb, norm_g, state):
    batch = rq.shape[0]
    full = lambda a: pl.BlockSpec(a.shape, lambda b, _nd=a.ndim: (0,) * _nd)
    st = pl.BlockSpec((None, N_HEADS_R, HEAD_DIM, HEAD_DIM), lambda b: (b, 0, 0, 0))
    lb2, ng2 = lb.reshape(1, -1), norm_g.reshape(1, -1)
    return pl.pallas_call(
        _hgrn_sample_kernel,
        grid=(batch,),
        in_specs=[full(rq), full(rf), full(ri), full(rg), full(lb2), full(ng2), st],
        out_specs=[full(rq), st],
        out_shape=[jax.ShapeDtypeStruct(rq.shape, F32), jax.ShapeDtypeStruct(state.shape, F32)],
        compiler_params=pltpu.CompilerParams(dimension_semantics=("arbitrary",)),
        name="hgrn_sample",
    )(rq, rf, ri, rg, lb2, ng2, state)


FFN_CHUNK = 256
ROW_TILE = 512
Q_TILE = 256
KV_STEP = 256
HGRN_CHUNK = 128
HGRN_TILE = 1024


def kernel(x_prompt, x_sample, cache_k, cache_v, cache_kidx, state_hgrn, page_table, w_in, w_out, hgrn_lb,
           hgrn_norm_g, ffn1_wg, ffn1_wu, ffn1_wd, ffn2_wg, ffn2_wu, ffn2_wd, ln1_g, ln1_b, ln2_g, ln2_b,
           ln3_g, ln3_b):
    assert w_in.shape[0] == DEPTH == 1
    batch, seq, d_model = x_prompt.shape
    dec_batch, dec_seq, _ = x_sample.shape
    assert dec_seq == 1
    n_pages = page_table.shape[1]
    l = 0

    lb = jnp.cumsum(jax.nn.softmax(hgrn_lb.astype(F32), axis=0), axis=0)[l]
    w1 = _ffn_weights(ffn1_wg[l], ffn1_wu[l], ffn1_wd[l])
    w2 = _ffn_weights(ffn2_wg[l], ffn2_wu[l], ffn2_wd[l])
    w_cat = _proj_weight(w_in[l])
    wo = w_out[l].astype(BF16)
    wo_a, wo_r = wo[:A_Q][_q_perm()], wo[A_Q:]

    xp = x_prompt.reshape(batch * seq, d_model)
    hp = _ffn(xp, w1, ln1_g[l], ln1_b[l], ROW_TILE)
    (q, k_t, v_t, kb_t, vb, qi, ki_t, kk_t, wi, rq, rf, ri, rg) = _proj(hp, w_cat, ROW_TILE, seq=seq)
    k = k_t.reshape(batch, N_KV_A, HEAD_DIM, seq).transpose(0, 3, 1, 2)
    v = v_t.reshape(batch, N_KV_A, HEAD_DIM, seq).transpose(0, 3, 1, 2)
    ki = ki_t.transpose(0, 2, 1)
    oa = _attn_prompt(q, qi, wi, kk_t, kb_t, vb, batch, Q_TILE)
    orr, sp = _hgrn_prompt(rq, rf, ri, rg, lb, hgrn_norm_g[l], batch, HGRN_CHUNK, HGRN_TILE)
    yp = _ffn(hp, w2, ln3_g[l], ln3_b[l], ROW_TILE, mix=(oa, orr, wo_a, wo_r, ln2_g[l], ln2_b[l]))

    xs = x_sample.reshape(dec_batch, d_model)
    hs = _ffn(xs, w1, ln1_g[l], ln1_b[l], dec_batch)
    (qs, ks, vs, _, _, qis, kis, _, wis, rqs, rfs, ris, rgs) = _proj(hs, w_cat, dec_batch)
    rows_pad = -(-(n_pages + 1) // SUBLANES) * SUBLANES
    cache_kiT = jnp.transpose(cache_kidx[l], (0, 2, 1))
    cache_kT = jnp.transpose(cache_k[l], (0, 2, 3, 1))
    cache_vT = jnp.transpose(cache_v[l], (0, 2, 3, 1))
    scores = _score_sample(page_table, qis.reshape(dec_batch, N_HEADS_IDX, HEAD_DIM), wis, kis, cache_kiT,
                           rows_pad)
    n_sel = min(TOPK_MAX, (n_pages * PAGE_SIZE + 1) // 4)
    bias = _select_sample(scores.reshape(dec_batch, rows_pad * PAGE_SIZE), n_sel)
    q_heads = qs.reshape(dec_batch, GROUP, 2, HEAD_DIM).transpose(0, 2, 1, 3).reshape(
        dec_batch, N_HEADS_A, HEAD_DIM)
    zeros = jnp.zeros_like(q_heads)
    q8 = jnp.concatenate([jnp.concatenate([q_heads[:, :GROUP], zeros[:, :GROUP]], -1),
                          jnp.concatenate([zeros[:, GROUP:], q_heads[:, GROUP:]], -1)], 1)
    oas = _attn_sample(page_table, q8, bias.reshape(dec_batch, rows_pad, PAGE_SIZE), ks, vs,
                       cache_kT, cache_vT).reshape(dec_batch, A_Q).astype(BF16)
    ors, ss = _hgrn_sample(rqs, rfs, ris, rgs, lb, hgrn_norm_g[l], state_hgrn[l])
    ys = _ffn(hs, w2, ln3_g[l], ln3_b[l], dec_batch,
              mix=(oas, ors.astype(BF16), wo_a, wo_r, ln2_g[l], ln2_b[l]))

    return (yp.reshape(batch, seq, d_model), ys.reshape(dec_batch, 1, d_model),
            k[None], v[None], ki[None], sp[None],
            ks.reshape(1, dec_batch, 1, N_KV_A, HEAD_DIM), vs.reshape(1, dec_batch, 1, N_KV_A, HEAD_DIM),
            kis.reshape(1, dec_batch, 1, HEAD_DIM), ss[None])
```
